```python
import jax, jax.numpy as jnp
from jax import lax
import numpy as np

D_MODEL = 2048
BATCH = 1
SEQ = 16384
DEPTH = 2

D_MIX = D_MODEL
HEAD_DIM = 128
ATT_HEADS = 8
ATT_W = ATT_HEADS * HEAD_DIM
CONV_GROUPS = 4
CONV_W = CONV_GROUPS * HEAD_DIM
SGU_HEADS = 4
SGU_W = SGU_HEADS * HEAD_DIM
CONV_K = 31
CHUNK = 128
Q_BLOCK = 128
PLE_DIM = 256
EPS = 1e-6
SPLIT_SIZES = (ATT_W, ATT_W, ATT_W, ATT_W, ATT_HEADS, CONV_W, CONV_W, CONV_W, SGU_W, SGU_W, SGU_W)
N_IN = 4 * ATT_W + ATT_HEADS + 3 * CONV_W + 3 * SGU_W

kernel_name = "hymba_style_conv_sgu_fox_hybrid"


def rmsnorm(x, g):
    xf = x.astype(jnp.float32)
    inv = lax.rsqrt(jnp.mean(xf * xf, axis=-1, keepdims=True) + EPS)
    return (xf * inv * g.astype(jnp.float32)).astype(x.dtype)


def layernorm(x, g, b):
    xf = x.astype(jnp.float32)
    mu = jnp.mean(xf, axis=-1, keepdims=True)
    var = jnp.mean(jnp.square(xf - mu), axis=-1, keepdims=True)
    y = (xf - mu) * lax.rsqrt(var + EPS) * g.astype(jnp.float32) + b.astype(jnp.float32)
    return y.astype(x.dtype)


def split_columns(proj):
    idx = np.cumsum(np.array(SPLIT_SIZES))[:-1].tolist()
    return jnp.split(proj, idx, axis=-1)


def conformer_conv(a, b, dw, dw_b, ln_g, ln_b, pw, pw_b):
    y = a * jax.nn.sigmoid(b)
    y = lax.conv_general_dilated(
        y, dw[:, None, :], window_strides=(1,), padding=[(CONV_K - 1, 0)],
        dimension_numbers=('NWC', 'WIO', 'NWC'), feature_group_count=CONV_W) + dw_b
    y = jax.nn.silu(layernorm(y, ln_g, ln_b))
    return y @ pw + pw_b


def spatial_gating(u, v, ln_g, ln_b, w_s, b_s):
    bsz, seq, _ = v.shape
    u = jax.nn.gelu(u, approximate=False)
    v = layernorm(jax.nn.gelu(v, approximate=False), ln_g, ln_b)
    v = v.reshape(bsz, seq // CHUNK, CHUNK, SGU_HEADS, HEAD_DIM)
    causal = jnp.tril(jnp.ones((CHUNK, CHUNK), dtype=bool))
    w = jnp.where(causal, w_s, 0)
    sv = jnp.einsum('hts,bnshd->bnthd', w, v) + b_s.T[:, :, None]
    return u * sv.reshape(bsz, seq, SGU_W)


def forgetting_attention(q, k, v, f_logit, b_f):
    bsz, seq, _ = q.shape
    q = q.reshape(bsz, seq, ATT_HEADS, HEAD_DIM) * (HEAD_DIM ** -0.5)
    k = k.reshape(bsz, seq, ATT_HEADS, HEAD_DIM)
    v = v.reshape(bsz, seq, ATT_HEADS, HEAD_DIM)
    log_f = jax.nn.log_sigmoid(f_logit.astype(jnp.float32) + b_f.astype(jnp.float32))
    c = jnp.cumsum(log_f, axis=1)
    c_k = jnp.transpose(c, (0, 2, 1))[:, :, None, :]
    key_pos = jnp.arange(seq)

    def block(i):
        start = i * Q_BLOCK
        qb = lax.dynamic_slice_in_dim(q, start, Q_BLOCK, axis=1)
        cq = lax.dynamic_slice_in_dim(c, start, Q_BLOCK, axis=1)
        s = jnp.einsum('bqhd,bkhd->bhqk', qb, k).astype(jnp.float32)
        s = s + (jnp.transpose(cq, (0, 2, 1))[:, :, :, None] - c_k)
        q_pos = start + jnp.arange(Q_BLOCK)
        mask = q_pos[:, None] >= key_pos[None, :]
        s = jnp.where(mask, s, -1e30)
        prob = jax.nn.softmax(s, axis=-1).astype(v.dtype)
        return jnp.einsum('bhqk,bkhd->bqhd', prob, v)

    out = lax.map(block, jnp.arange(seq // Q_BLOCK))
    return jnp.moveaxis(out, 0, 1).reshape(bsz, seq, ATT_W)


def setup_inputs(seed: int = 0) -> dict:
    key = jax.random.key(seed)
    ks = jax.random.split(key, 24)
    f32 = jnp.float32
    nrm = lambda k, shape, scale: jax.random.normal(k, shape, f32) * scale
    return {
        "x": nrm(ks[0], (BATCH, SEQ, D_MODEL), 1.0),
        "p": nrm(ks[1], (DEPTH, BATCH, SEQ, PLE_DIM), 1.0),
        "norm_pre": 1.0 + nrm(ks[2], (DEPTH, D_MODEL), 0.02),
        "w_in": nrm(ks[3], (DEPTH, D_MODEL, N_IN), D_MODEL ** -0.5),
        "b_f": jnp.broadcast_to(jnp.linspace(1.0, 6.0, ATT_HEADS, dtype=f32), (DEPTH, ATT_HEADS)) + nrm(ks[4], (DEPTH, ATT_HEADS), 0.1),
        "conv_dw": nrm(ks[5], (DEPTH, CONV_K, CONV_W), CONV_K ** -0.5),
        "conv_dw_b": nrm(ks[6], (DEPTH, CONV_W), 0.01),
        "conv_ln_g": 1.0 + nrm(ks[7], (DEPTH, CONV_W), 0.02),
        "conv_ln_b": nrm(ks[8], (DEPTH, CONV_W), 0.01),
        "conv_pw": nrm(ks[9], (DEPTH, CONV_W, CONV_W), CONV_W ** -0.5),
        "conv_pw_b": nrm(ks[10], (DEPTH, CONV_W), 0.01),
        "sgu_ln_g": 1.0 + nrm(ks[11], (DEPTH, SGU_W), 0.02),
        "sgu_ln_b": nrm(ks[12], (DEPTH, SGU_W), 0.01),
        "sgu_w": nrm(ks[13], (DEPTH, SGU_HEADS, CHUNK, CHUNK), CHUNK ** -0.5),
        "sgu_b": 1.0 + nrm(ks[14], (DEPTH, SGU_HEADS, CHUNK), 0.01),
        "w_out": nrm(ks[15], (DEPTH, D_MIX, D_MODEL), D_MIX ** -0.5),
        "norm_post": 1.0 + nrm(ks[16], (DEPTH, D_MODEL), 0.02),
        "w_pg": nrm(ks[17], (DEPTH, D_MODEL, D_MODEL), D_MODEL ** -0.5),
        "w_pp": nrm(ks[18], (DEPTH, PLE_DIM, D_MODEL), PLE_DIM ** -0.5),
    }


def reference(x, p, norm_pre, w_in, b_f, conv_dw, conv_dw_b, conv_ln_g, conv_ln_b, conv_pw, conv_pw_b,
              sgu_ln_g, sgu_ln_b, sgu_w, sgu_b, w_out, norm_post, w_pg, w_pp):
    h = x
    for i in range(DEPTH):
        xn = rmsnorm(h, norm_pre[i])
        proj = xn @ w_in[i]
        (q, k, v, z_att, f_logit, glu_a, glu_b, z_conv, u_sgu, v_sgu, z_sgu) = split_columns(proj)
        y_conv = conformer_conv(glu_a, glu_b, conv_dw[i], conv_dw_b[i], conv_ln_g[i], conv_ln_b[i],
                                conv_pw[i], conv_pw_b[i]) * jax.nn.silu(z_conv)
        y_sgu = spatial_gating(u_sgu, v_sgu, sgu_ln_g[i], sgu_ln_b[i], sgu_w[i], sgu_b[i]) * jax.nn.silu(z_sgu)
        y_att = forgetting_attention(q, k, v, f_logit, b_f[i]) * jax.nn.silu(z_att)
        y = jnp.concatenate([y_conv, y_sgu, y_att], axis=-1) @ w_out[i]
        h = h + rmsnorm(y, norm_post[i])
        h = h + jax.nn.sigmoid(h @ w_pg[i]) * (p[i] @ w_pp[i])
    return h
```

```python
import functools

import jax
import jax.numpy as jnp
from jax import lax
from jax.experimental import pallas as pl
from jax.experimental.pallas import tpu as pltpu

F32 = jnp.float32
BF16 = jnp.bfloat16

D_MODEL = 2048
HEAD_DIM = 128
ATT_HEADS = 8
ATT_W = ATT_HEADS * HEAD_DIM
CONV_W = 512
SGU_HEADS = 4
SGU_W = SGU_HEADS * HEAD_DIM
CONV_K = 31
CHUNK = 128
PLE_DIM = 256
EPS = 1e-6
N_MAIN = 4 * ATT_W + 3 * CONV_W + 3 * SGU_W

LANES = 128
SUBLANES = 8
VMEM_BYTES_V7X = 64 * 1024 * 1024

PROJ_TM = 1024
PROJ_TN = 1024
FG_T = 256
BR_TM = 256
CONV_HALO = 32

ATT_TQ = 512
ATT_TK = FG_T
OUT_TM = 256
MASK_VALUE = -1e30


def _vmem_limit(nbytes):
    assert nbytes < VMEM_BYTES_V7X
    return int(nbytes)


def _proj_kernel(x_ref, g_ref, w_ref, wf_ref, cs_ref, o_ref, f_ref, xn_ref):
    @pl.when(pl.program_id(1) == 0)
    def _():
        x = x_ref[...]
        inv = lax.rsqrt(jnp.mean(x * x, axis=-1, keepdims=True) + EPS)
        xn_ref[...] = (x * inv * g_ref[...]).astype(BF16)
        f_ref[...] = jnp.dot(xn_ref[...], wf_ref[...], preferred_element_type=F32)

    acc = jnp.dot(xn_ref[...], w_ref[...], preferred_element_type=F32)
    o_ref[...] = (acc * cs_ref[...]).astype(BF16)


def _proj(h, g, w_main, w_f, colscale):
    s = h.shape[0]
    grid = (s // PROJ_TM, N_MAIN // PROJ_TN)
    return pl.pallas_call(
        _proj_kernel,
        grid=grid,
        in_specs=[
            pl.BlockSpec((PROJ_TM, D_MODEL), lambda i, j: (i, 0)),
            pl.BlockSpec((1, D_MODEL), lambda i, j: (0, 0)),
            pl.BlockSpec((D_MODEL, PROJ_TN), lambda i, j: (0, j)),
            pl.BlockSpec((D_MODEL, LANES), lambda i, j: (0, 0)),
            pl.BlockSpec((1, PROJ_TN), lambda i, j: (0, j)),
        ],
        out_specs=[
            pl.BlockSpec((PROJ_TM, PROJ_TN), lambda i, j: (i, j)),
            pl.BlockSpec((PROJ_TM, LANES), lambda i, j: (i, 0)),
        ],
        out_shape=[
            jax.ShapeDtypeStruct((s, N_MAIN), BF16),
            jax.ShapeDtypeStruct((s, LANES), F32),
        ],
        scratch_shapes=[pltpu.VMEM((PROJ_TM, D_MODEL), BF16)],
        compiler_params=pltpu.CompilerParams(
            dimension_semantics=("arbitrary", "arbitrary"),
            vmem_limit_bytes=_vmem_limit(48 * 1024 * 1024),
        ),
        name="proj",
    )(h, g, w_main, w_f, colscale)


def _split3(x):
    p1 = x.astype(BF16)
    r1 = x - p1.astype(F32)
    p2 = r1.astype(BF16)
    p3 = (r1 - p2.astype(F32)).astype(BF16)
    return p1, p2, p3


def _forget_kernel(f_ref, bf_ref, tri_ref, nr_ref, cs_ref, carry_ref):
    @pl.when(pl.program_id(0) == 0)
    def _():
        carry_ref[...] = jnp.zeros_like(carry_ref)

    lane = lax.broadcasted_iota(jnp.int32, (FG_T, LANES), 1)
    row = lax.broadcasted_iota(jnp.int32, (FG_T, LANES), 0)
    z = f_ref[...] + bf_ref[...]
    logf = jnp.minimum(z, 0.0) - jnp.log1p(jnp.exp(-jnp.abs(z)))
    logf = jnp.where(lane < ATT_HEADS, logf, 0.0)
    first = logf[0:1, :]
    p1, p2, p3 = _split3(jnp.where(row == 0, 0.0, logf))
    tri = tri_ref[...]
    r = (jnp.dot(tri, p1, preferred_element_type=F32)
         + jnp.dot(tri, p2, preferred_element_type=F32)
         + jnp.dot(tri, p3, preferred_element_type=F32))
    c_first = carry_ref[0:1, :] + first
    cs_ref[...] = jnp.broadcast_to(c_first[None], cs_ref.shape)
    carry_ref[...] = jnp.broadcast_to(c_first + r[FG_T - 1:FG_T, :], carry_ref.shape)

    n1, n2, n3 = _split3(-r)
    packed = jnp.where(
        lane < ATT_HEADS, n1.astype(F32),
        jnp.where(lane < 2 * ATT_HEADS, pltpu.roll(n2.astype(F32), ATT_HEADS, 1),
                  jnp.where(lane < 3 * ATT_HEADS, pltpu.roll(n3.astype(F32), 2 * ATT_HEADS, 1), 0.0)))
    nr_ref[...] = packed.astype(BF16)


def _forget(flog, bf_pad, tri):
    s = flog.shape[0]
    nblk = s // FG_T
    return pl.pallas_call(
        _forget_kernel,
        grid=(nblk,),
        in_specs=[
            pl.BlockSpec((FG_T, LANES), lambda j: (j, 0)),
            pl.BlockSpec((1, LANES), lambda j: (0, 0)),
            pl.BlockSpec((FG_T, FG_T), lambda j: (0, 0)),
        ],
        out_specs=[
            pl.BlockSpec((FG_T, LANES), lambda j: (j, 0)),
            pl.BlockSpec((1, SUBLANES, LANES), lambda j: (j, 0, 0)),
        ],
        out_shape=[
            jax.ShapeDtypeStruct((s, LANES), BF16),
            jax.ShapeDtypeStruct((nblk, SUBLANES, LANES), F32),
        ],
        scratch_shapes=[pltpu.VMEM((SUBLANES, LANES), F32)],
        compiler_params=pltpu.CompilerParams(dimension_semantics=("arbitrary",)),
        name="forget",
    )(flog, bf_pad, tri)


def _layernorm(x, g, b):
    mu = jnp.mean(x, axis=-1, keepdims=True)
    xc = x - mu
    var = jnp.mean(xc * xc, axis=-1, keepdims=True)
    return xc * lax.rsqrt(var + EPS) * g + b


def _silu(x):
    return x * jax.nn.sigmoid(x)


def _gelu(x):
    return 0.5 * x * (1.0 + lax.erf(x * (2.0 ** -0.5)))


def _branch_kernel(a_ref, b_ref, ah_ref, bh_ref, zc_ref, u_ref, v_ref, zs_ref,
                   dw_ref, dwb_ref, cg_ref, cb_ref, pw_ref, pwb_ref,
                   sg_ref, sb_ref, ws_ref, bs_ref,
                   o_ref, y_ref, c_ref):
    i = pl.program_id(0)

    ah = ah_ref[...].astype(F32)
    bh = bh_ref[...].astype(F32)
    hist = ah * jax.nn.sigmoid(bh)
    y_ref[0:CONV_HALO, :] = jnp.where(i == 0, 0.0, hist)
    a = a_ref[...].astype(F32)
    b = b_ref[...].astype(F32)
    y_ref[CONV_HALO:, :] = a * jax.nn.sigmoid(b)

    base = CONV_HALO - (CONV_K - 1)
    for lb in range(CONV_W // LANES):
        lsl = slice(lb * LANES, (lb + 1) * LANES)
        taps = [jnp.broadcast_to(dw_ref[j:j + 1, lsl], (SUBLANES, LANES)) for j in range(CONV_K)]
        bias = jnp.broadcast_to(dwb_ref[:, lsl], (SUBLANES, LANES))

        for r0 in range(0, BR_TM, SUBLANES):
            acc = bias
            for j in range(CONV_K):
                acc = acc + taps[j] * y_ref[r0 + base + j:r0 + base + j + SUBLANES, lsl]
            c_ref[r0:r0 + SUBLANES, lsl] = acc

    yc = _silu(_layernorm(c_ref[...], cg_ref[...], cb_ref[...]))
    yc = jnp.dot(yc.astype(BF16), pw_ref[...], preferred_element_type=F32) + pwb_ref[...]
    zc = zc_ref[...].astype(F32)
    o_ref[:, 0:CONV_W] = (yc * _silu(zc)).astype(BF16)

    vln = _layernorm(_gelu(v_ref[...].astype(F32)), sg_ref[...], sb_ref[...]).astype(BF16)
    tr = lax.broadcasted_iota(jnp.int32, (CHUNK, CHUNK), 0)
    tc = lax.broadcasted_iota(jnp.int32, (CHUNK, CHUNK), 1)
    for hh in range(SGU_HEADS):
        w = jnp.where(tr >= tc, ws_ref[hh], 0.0).astype(BF16)
        for c in range(BR_TM // CHUNK):
            rs = slice(c * CHUNK, (c + 1) * CHUNK)
            cs = slice(hh * HEAD_DIM, (hh + 1) * HEAD_DIM)
            sv = jnp.dot(w, vln[rs, cs], preferred_element_type=F32) + bs_ref[hh]
            c_ref[rs, cs] = sv
    gu = _gelu(u_ref[...].astype(F32))
    zs = zs_ref[...].astype(F32)
    o_ref[:, CONV_W:] = (gu * c_ref[...] * _silu(zs)).astype(BF16)


def _branch(proj, dw, dwb, cg, cb, pw, pwb, sg, sb, ws, bsb):
    s = proj.shape[0]
    w = CONV_W
    c0 = 4 * ATT_W // w
    hpb = BR_TM // CONV_HALO

    def col(k):
        return pl.BlockSpec((BR_TM, w), lambda i, k=k: (i, c0 + k))

    def halo(k):
        return pl.BlockSpec((CONV_HALO, w), lambda i, k=k: (jnp.maximum(i * hpb - 1, 0), c0 + k))

    def full(shape):
        return pl.BlockSpec(shape, lambda i: (0,) * len(shape))

    return pl.pallas_call(
        _branch_kernel,
        grid=(s // BR_TM,),
        in_specs=[
            col(0), col(1), halo(0), halo(1), col(2), col(3), col(4), col(5),
            full((CONV_K, w)), full((1, w)), full((1, w)), full((1, w)), full((w, w)), full((1, w)),
            full((1, w)), full((1, w)), full((SGU_HEADS, CHUNK, CHUNK)), full((SGU_HEADS, CHUNK, HEAD_DIM)),
        ],
        out_specs=pl.BlockSpec((BR_TM, 2 * w), lambda i: (i, 0)),
        out_shape=jax.ShapeDtypeStruct((s, 2 * w), BF16),
        scratch_shapes=[
            pltpu.VMEM((CONV_HALO + BR_TM, w), F32),
            pltpu.VMEM((BR_TM, w), F32),
        ],
        compiler_params=pltpu.CompilerParams(
            dimension_semantics=("arbitrary",),
            vmem_limit_bytes=_vmem_limit(40 * 1024 * 1024),
        ),
        name="branch",
    )(proj, proj, proj, proj, proj, proj, proj, proj, dw, dwb, cg, cb, pw, pwb, sg, sb, ws, bsb)


def _attn_kernel(cs_ref, q_ref, k_ref, v_ref, nr_ref, z_ref, o_ref,
                 vt_ref, qa_ref, acc_ref, m_ref, l_ref, *, nkb):
    h = pl.program_id(0)
    i = pl.program_id(1)

    @pl.when(i == 0)
    def _():
        def xpose(c, carry):
            blk = v_ref[pl.ds(pl.multiple_of(c * ATT_TK, ATT_TK), ATT_TK), :]
            vt_ref[c] = blk.astype(F32).T.astype(BF16)
            return carry
        lax.fori_loop(0, nkb, xpose, 0)

    qa_ref[0:HEAD_DIM, :] = q_ref[...].astype(F32).T.astype(BF16)
    rr = lax.broadcasted_iota(jnp.int32, (LANES, ATT_TQ), 0)
    pick = (rr == h) | (rr == h + ATT_HEADS) | (rr == h + 2 * ATT_HEADS)
    qa_ref[HEAD_DIM:, :] = jnp.where(pick, 1.0, 0.0).astype(BF16)
    m_ref[...] = jnp.full_like(m_ref, MASK_VALUE)
    l_ref[...] = jnp.zeros_like(l_ref)
    acc_ref[...] = jnp.zeros_like(acc_ref)

    kpq = ATT_TQ // ATT_TK
    c_q0 = cs_ref[h * nkb + i * kpq]

    def step(j, masked):
        koff = pl.multiple_of(j * ATT_TK, ATT_TK)
        ka = jnp.concatenate([k_ref[pl.ds(koff, ATT_TK), :], nr_ref[pl.ds(koff, ATT_TK), :]], axis=1)
        s = jnp.dot(ka, qa_ref[...], preferred_element_type=F32)
        if masked:
            kpos = j * ATT_TK + lax.broadcasted_iota(jnp.int32, (ATT_TK, ATT_TQ), 0)
            qpos = i * ATT_TQ + lax.broadcasted_iota(jnp.int32, (ATT_TK, ATT_TQ), 1)
            s = jnp.where(kpos <= qpos, s, MASK_VALUE)
        d = c_q0 - cs_ref[h * nkb + j]
        m_old = m_ref[...]
        m_new = jnp.maximum(m_old, jnp.max(s, axis=0, keepdims=True) + d)
        alpha = jnp.exp(m_old - m_new)
        p = jnp.exp(s - (m_new - d))
        l_ref[...] = alpha * l_ref[...] + jnp.sum(p, axis=0, keepdims=True)
        pv = jnp.dot(vt_ref[j], p.astype(BF16), preferred_element_type=F32)
        acc_ref[...] = acc_ref[...] * alpha + pv
        m_ref[...] = m_new

    def body(j, carry):
        step(j, False)
        return carry

    lax.fori_loop(0, i * kpq, body, 0)
    for dj in range(kpq):
        step(i * kpq + dj, True)

    out = (acc_ref[...] / l_ref[...]).T
    z = z_ref[...].astype(F32)
    o_ref[...] = (out * _silu(z)).astype(BF16)


def _attn(cstart, proj, nr):
    s = proj.shape[0]
    nkb = s // ATT_TK
    nh = ATT_HEADS
    grid_spec = pltpu.PrefetchScalarGridSpec(
        num_scalar_prefetch=1,
        grid=(nh, s // ATT_TQ),
        in_specs=[
            pl.BlockSpec((ATT_TQ, HEAD_DIM), lambda h, i, cs: (i, h)),
            pl.BlockSpec((s, HEAD_DIM), lambda h, i, cs: (0, nh + h)),
            pl.BlockSpec((s, HEAD_DIM), lambda h, i, cs: (0, 2 * nh + h)),
            pl.BlockSpec((s, LANES), lambda h, i, cs: (0, 0)),
            pl.BlockSpec((ATT_TQ, HEAD_DIM), lambda h, i, cs: (i, 3 * nh + h)),
        ],
        out_specs=pl.BlockSpec((ATT_TQ, HEAD_DIM), lambda h, i, cs: (i, h)),
        scratch_shapes=[
            pltpu.VMEM((nkb, HEAD_DIM, ATT_TK), BF16),
            pltpu.VMEM((2 * HEAD_DIM, ATT_TQ), BF16),
            pltpu.VMEM((HEAD_DIM, ATT_TQ), F32),
            pltpu.VMEM((1, ATT_TQ), F32),
            pltpu.VMEM((1, ATT_TQ), F32),
        ],
    )
    return pl.pallas_call(
        functools.partial(_attn_kernel, nkb=nkb),
        grid_spec=grid_spec,
        out_shape=jax.ShapeDtypeStruct((s, ATT_W), BF16),
        compiler_params=pltpu.CompilerParams(
            dimension_semantics=("arbitrary", "arbitrary"),
            vmem_limit_bytes=_vmem_limit(48 * 1024 * 1024),
        ),
        name="attn",
    )(cstart, proj, proj, proj, nr, proj)


def _out_kernel(h_ref, ycs_ref, ya_ref, p_ref, wo_ref, gp_ref, wpg_ref, wpp_ref, o_ref):
    ycat = jnp.concatenate([ycs_ref[...], ya_ref[...]], axis=1)
    y = jnp.dot(ycat, wo_ref[...], preferred_element_type=F32)
    inv = lax.rsqrt(jnp.mean(y * y, axis=-1, keepdims=True) + EPS)
    h1 = h_ref[...] + y * inv * gp_ref[...]
    gate = jax.nn.sigmoid(jnp.dot(h1.astype(BF16), wpg_ref[...], preferred_element_type=F32))
    pp = jnp.dot(p_ref[...].astype(BF16), wpp_ref[...], preferred_element_type=F32)
    o_ref[...] = h1 + gate * pp


def _out(h, ycs, yatt, p, wo, gp, wpg, wpp):
    s = h.shape[0]

    def const(shape):
        return pl.BlockSpec(shape, lambda i: (0, 0), pipeline_mode=pl.Buffered(1))

    return pl.pallas_call(
        _out_kernel,
        grid=(s // OUT_TM,),
        in_specs=[
            pl.BlockSpec((OUT_TM, D_MODEL), lambda i: (i, 0)),
            pl.BlockSpec((OUT_TM, CONV_W + SGU_W), lambda i: (i, 0)),
            pl.BlockSpec((OUT_TM, ATT_W), lambda i: (i, 0)),
            pl.BlockSpec((OUT_TM, PLE_DIM), lambda i: (i, 0)),
            const((D_MODEL, D_MODEL)),
            const((1, D_MODEL)),
            const((D_MODEL, D_MODEL)),
            const((PLE_DIM, D_MODEL)),
        ],
        out_specs=pl.BlockSpec((OUT_TM, D_MODEL), lambda i: (i, 0)),
        out_shape=jax.ShapeDtypeStruct((s, D_MODEL), F32),
        compiler_params=pltpu.CompilerParams(
            dimension_semantics=("arbitrary",),
            vmem_limit_bytes=_vmem_limit(52 * 1024 * 1024),
        ),
        name="out",
    )(h, ycs, yatt, p, wo, gp, wpg, wpp)


def kernel(x, p, norm_pre, w_in, b_f, conv_dw, conv_dw_b, conv_ln_g, conv_ln_b, conv_pw, conv_pw_b,
           sgu_ln_g, sgu_ln_b, sgu_w, sgu_b, w_out, norm_post, w_pg, w_pp):
    bsz, seq, _ = x.shape
    depth = w_in.shape[0]
    assert bsz == 1 and seq % PROJ_TM == 0 and seq % ATT_TQ == 0

    n_att = 4 * ATT_W
    colscale = jnp.concatenate(
        [jnp.full((1, ATT_W), HEAD_DIM ** -0.5, F32), jnp.ones((1, N_MAIN - ATT_W), F32)], axis=1)
    tri = jnp.tril(jnp.ones((FG_T, FG_T), BF16))
    row = lambda v: v.reshape(1, -1)

    h = x[0]
    for i in range(depth):
        w_main = jnp.concatenate([w_in[i, :, :n_att], w_in[i, :, n_att + ATT_HEADS:]], axis=1).astype(BF16)
        w_f = jnp.pad(w_in[i, :, n_att:n_att + ATT_HEADS], ((0, 0), (0, LANES - ATT_HEADS))).astype(BF16)
        bf_pad = jnp.pad(b_f[i], (0, LANES - ATT_HEADS)).reshape(1, LANES)
        bsb = jnp.broadcast_to(sgu_b[i][:, :, None], (SGU_HEADS, CHUNK, HEAD_DIM))

        proj, flog = _proj(h, row(norm_pre[i]), w_main, w_f, colscale)
        nr, cs = _forget(flog, bf_pad, tri)
        cstart = cs[:, 0, :ATT_HEADS].T.reshape(-1)
        ycs = _branch(proj, conv_dw[i], row(conv_dw_b[i]), row(conv_ln_g[i]), row(conv_ln_b[i]),
                      conv_pw[i].astype(BF16), row(conv_pw_b[i]),
                      row(sgu_ln_g[i]), row(sgu_ln_b[i]), sgu_w[i], bsb)
        yatt = _attn(cstart, proj, nr)
        h = _out(h, ycs, yatt, p[i, 0], w_out[i].astype(BF16), row(norm_post[i]),
                 w_pg[i].astype(BF16), w_pp[i].astype(BF16))
    return h[None]
```

```python
import functools

import jax
import jax.numpy as jnp
from jax import lax
from jax.experimental import pallas as pl
from jax.experimental.pallas import tpu as pltpu

F32 = jnp.float32
BF16 = jnp.bfloat16

D_MODEL = 2048
HEAD_DIM = 128
ATT_HEADS = 8
ATT_W = ATT_HEADS * HEAD_DIM
CONV_W = 512
SGU_HEADS = 4
SGU_W = SGU_HEADS * HEAD_DIM
CONV_K = 31
CHUNK = 128
PLE_DIM = 256
EPS = 1e-6
N_MAIN = 4 * ATT_W + 3 * CONV_W + 3 * SGU_W

LANES = 128
SUBLANES = 8
VMEM_BYTES_V7X = 64 * 1024 * 1024

PROJ_TM = 1024
PROJ_TN = 1024
FG_T = 512
BR_TM = 256
CONV_HALO = 32
ATT_TK = FG_T
ATT_TG = ATT_TK
ATT_GROUPS = 2
ATT_TQ = ATT_GROUPS * ATT_TG
OUT_TM = 256
MASK_VALUE = -1e30
LOG2E = 1.4426950408889634


def _vmem_limit(nbytes):
    assert nbytes < VMEM_BYTES_V7X
    return int(nbytes)


def _proj_kernel(x_ref, g_ref, w_ref, wf_ref, cs_ref, o_ref, f_ref, xn_ref):
    @pl.when(pl.program_id(1) == 0)
    def _():
        x = x_ref[...]
        inv = lax.rsqrt(jnp.mean(x * x, axis=-1, keepdims=True) + EPS)
        xn_ref[...] = (x * inv * g_ref[...]).astype(BF16)
        f_ref[...] = jnp.dot(xn_ref[...], wf_ref[...], preferred_element_type=F32)

    acc = jnp.dot(xn_ref[...], w_ref[...], preferred_element_type=F32)
    o_ref[...] = (acc * cs_ref[...]).astype(BF16)


def _proj(h, g, w_main, w_f, colscale):
    s = h.shape[0]
    grid = (s // PROJ_TM, N_MAIN // PROJ_TN)
    return pl.pallas_call(
        _proj_kernel,
        grid=grid,
        in_specs=[
            pl.BlockSpec((PROJ_TM, D_MODEL), lambda i, j: (i, 0)),
            pl.BlockSpec((1, D_MODEL), lambda i, j: (0, 0)),
            pl.BlockSpec((D_MODEL, PROJ_TN), lambda i, j: (0, j)),
            pl.BlockSpec((D_MODEL, LANES), lambda i, j: (0, 0)),
            pl.BlockSpec((1, PROJ_TN), lambda i, j: (0, j)),
        ],
        out_specs=[
            pl.BlockSpec((PROJ_TM, PROJ_TN), lambda i, j: (i, j)),
            pl.BlockSpec((PROJ_TM, LANES), lambda i, j: (i, 0)),
        ],
        out_shape=[
            jax.ShapeDtypeStruct((s, N_MAIN), BF16),
            jax.ShapeDtypeStruct((s, LANES), F32),
        ],
        scratch_shapes=[pltpu.VMEM((PROJ_TM, D_MODEL), BF16)],
        compiler_params=pltpu.CompilerParams(
            dimension_semantics=("arbitrary", "arbitrary"),
            vmem_limit_bytes=_vmem_limit(48 * 1024 * 1024),
        ),
        name="proj",
    )(h, g, w_main, w_f, colscale)


def _split3(x):
    p1 = x.astype(BF16)
    r1 = x - p1.astype(F32)
    p2 = r1.astype(BF16)
    p3 = (r1 - p2.astype(F32)).astype(BF16)
    return p1, p2, p3


def _forget_kernel(f_ref, bf_ref, tri_ref, nr_ref, cs_ref, carry_ref):
    @pl.when(pl.program_id(0) == 0)
    def _():
        carry_ref[...] = jnp.zeros_like(carry_ref)

    lane = lax.broadcasted_iota(jnp.int32, (FG_T, LANES), 1)
    row = lax.broadcasted_iota(jnp.int32, (FG_T, LANES), 0)
    z = f_ref[...] + bf_ref[...]
    logf = (jnp.minimum(z, 0.0) - jnp.log1p(jnp.exp(-jnp.abs(z)))) * LOG2E
    logf = jnp.where(lane < ATT_HEADS, logf, 0.0)
    first = logf[0:1, :]
    p1, p2, p3 = _split3(jnp.where(row == 0, 0.0, logf))
    tri = tri_ref[...]
    r = (jnp.dot(tri, p1, preferred_element_type=F32)
         + jnp.dot(tri, p2, preferred_element_type=F32)
         + jnp.dot(tri, p3, preferred_element_type=F32))
    c_first = carry_ref[0:1, :] + first
    cs_ref[...] = jnp.broadcast_to(c_first[None], cs_ref.shape)
    carry_ref[...] = jnp.broadcast_to(c_first + r[FG_T - 1:FG_T, :], carry_ref.shape)

    n1, n2, n3 = _split3(-r)
    packed = jnp.where(
        lane < ATT_HEADS, n1.astype(F32),
        jnp.where(lane < 2 * ATT_HEADS, pltpu.roll(n2.astype(F32), ATT_HEADS, 1),
                  jnp.where(lane < 3 * ATT_HEADS, pltpu.roll(n3.astype(F32), 2 * ATT_HEADS, 1), 0.0)))
    nr_ref[...] = packed.astype(BF16)


def _forget(flog, bf_pad, tri):
    s = flog.shape[0]
    nblk = s // FG_T
    return pl.pallas_call(
        _forget_kernel,
        grid=(nblk,),
        in_specs=[
            pl.BlockSpec((FG_T, LANES), lambda j: (j, 0)),
            pl.BlockSpec((1, LANES), lambda j: (0, 0)),
            pl.BlockSpec((FG_T, FG_T), lambda j: (0, 0)),
        ],
        out_specs=[
            pl.BlockSpec((FG_T, LANES), lambda j: (j, 0)),
            pl.BlockSpec((1, SUBLANES, LANES), lambda j: (j, 0, 0)),
        ],
        out_shape=[
            jax.ShapeDtypeStruct((s, LANES), BF16),
            jax.ShapeDtypeStruct((nblk, SUBLANES, LANES), F32),
        ],
        scratch_shapes=[pltpu.VMEM((SUBLANES, LANES), F32)],
        compiler_params=pltpu.CompilerParams(dimension_semantics=("arbitrary",)),
        name="forget",
    )(flog, bf_pad, tri)


def _layernorm(x, g, b):
    mu = jnp.mean(x, axis=-1, keepdims=True)
    xc = x - mu
    var = jnp.mean(xc * xc, axis=-1, keepdims=True)
    return xc * lax.rsqrt(var + EPS) * g + b


def _silu(x):
    return x * jax.nn.sigmoid(x)


def _gelu(x):
    return 0.5 * x * (1.0 + lax.erf(x * (2.0 ** -0.5)))


def _branch_kernel(a_ref, b_ref, ah_ref, bh_ref, zc_ref, u_ref, v_ref, zs_ref,
                   dw_ref, dwb_ref, cg_ref, cb_ref, pw_ref, pwb_ref,
                   sg_ref, sb_ref, ws_ref, bs_ref,
                   o_ref, y_ref, c_ref):
    i = pl.program_id(0)

    ah = ah_ref[...].astype(F32)
    bh = bh_ref[...].astype(F32)
    hist = ah * jax.nn.sigmoid(bh)
    y_ref[0:CONV_HALO, :] = jnp.where(i == 0, 0.0, hist)
    a = a_ref[...].astype(F32)
    b = b_ref[...].astype(F32)
    y_ref[CONV_HALO:, :] = a * jax.nn.sigmoid(b)

    base = CONV_HALO - (CONV_K - 1)
    for lb in range(CONV_W // LANES):
        lsl = slice(lb * LANES, (lb + 1) * LANES)
        taps = [jnp.broadcast_to(dw_ref[j:j + 1, lsl], (SUBLANES, LANES)) for j in range(CONV_K)]
        bias = jnp.broadcast_to(dwb_ref[:, lsl], (SUBLANES, LANES))

        for r0 in range(0, BR_TM, SUBLANES):
            acc = bias
            for j in range(CONV_K):
                acc = acc + taps[j] * y_ref[r0 + base + j:r0 + base + j + SUBLANES, lsl]
            c_ref[r0:r0 + SUBLANES, lsl] = acc

    yc = _silu(_layernorm(c_ref[...], cg_ref[...], cb_ref[...]))
    yc = jnp.dot(yc.astype(BF16), pw_ref[...], preferred_element_type=F32) + pwb_ref[...]
    zc = zc_ref[...].astype(F32)
    o_ref[:, 0:CONV_W] = (yc * _silu(zc)).astype(BF16)

    vln = _layernorm(_gelu(v_ref[...].astype(F32)), sg_ref[...], sb_ref[...]).astype(BF16)
    tr = lax.broadcasted_iota(jnp.int32, (CHUNK, CHUNK), 0)
    tc = lax.broadcasted_iota(jnp.int32, (CHUNK, CHUNK), 1)
    for hh in range(SGU_HEADS):
        w = jnp.where(tr >= tc, ws_ref[hh], 0.0).astype(BF16)
        for c in range(BR_TM // CHUNK):
            rs = slice(c * CHUNK, (c + 1) * CHUNK)
            cs = slice(hh * HEAD_DIM, (hh + 1) * HEAD_DIM)
            sv = jnp.dot(w, vln[rs, cs], preferred_element_type=F32) + bs_ref[hh]
            c_ref[rs, cs] = sv
    gu = _gelu(u_ref[...].astype(F32))
    zs = zs_ref[...].astype(F32)
    o_ref[:, CONV_W:] = (gu * c_ref[...] * _silu(zs)).astype(BF16)


def _branch(proj, dw, dwb, cg, cb, pw, pwb, sg, sb, ws, bsb):
    s = proj.shape[0]
    w = CONV_W
    c0 = 4 * ATT_W // w
    hpb = BR_TM // CONV_HALO

    def col(k):
        return pl.BlockSpec((BR_TM, w), lambda i, k=k: (i, c0 + k))

    def halo(k):
        return pl.BlockSpec((CONV_HALO, w), lambda i, k=k: (jnp.maximum(i * hpb - 1, 0), c0 + k))

    def full(shape):
        return pl.BlockSpec(shape, lambda i: (0,) * len(shape))

    return pl.pallas_call(
        _branch_kernel,
        grid=(s // BR_TM,),
        in_specs=[
            col(0), col(1), halo(0), halo(1), col(2), col(3), col(4), col(5),
            full((CONV_K, w)), full((1, w)), full((1, w)), full((1, w)), full((w, w)), full((1, w)),
            full((1, w)), full((1, w)), full((SGU_HEADS, CHUNK, CHUNK)), full((SGU_HEADS, CHUNK, HEAD_DIM)),
        ],
        out_specs=pl.BlockSpec((BR_TM, 2 * w), lambda i: (i, 0)),
        out_shape=jax.ShapeDtypeStruct((s, 2 * w), BF16),
        scratch_shapes=[
            pltpu.VMEM((CONV_HALO + BR_TM, w), F32),
            pltpu.VMEM((BR_TM, w), F32),
        ],
        compiler_params=pltpu.CompilerParams(
            dimension_semantics=("arbitrary",),
            vmem_limit_bytes=_vmem_limit(40 * 1024 * 1024),
        ),
        name="branch",
    )(proj, proj, proj, proj, proj, proj, proj, proj, dw, dwb, cg, cb, pw, pwb, sg, sb, ws, bsb)


def _attn_kernel(cs_ref, q_ref, k_ref, v_ref, nr_ref, z_ref, o_ref,
                 vt_ref, qa_ref, s_ref, acc_ref, m_ref, l_ref, *, nkb):
    h = pl.program_id(0)
    i = pl.program_id(1)

    @pl.when(i == 0)
    def _():
        def xpose(c, carry):
            blk = v_ref[pl.ds(pl.multiple_of(c * ATT_TK, ATT_TK), ATT_TK), :]
            vt_ref[c] = blk.astype(F32).T.astype(BF16)
            return carry
        lax.fori_loop(0, nkb, xpose, 0)

    qa_ref[0:HEAD_DIM, :] = q_ref[...].astype(F32).T.astype(BF16)
    rr = lax.broadcasted_iota(jnp.int32, (LANES, ATT_TQ), 0)
    pick = (rr == h) | (rr == h + ATT_HEADS) | (rr == h + 2 * ATT_HEADS)
    qa_ref[HEAD_DIM:, :] = jnp.where(pick, 1.0, 0.0).astype(BF16)
    m_ref[...] = jnp.full_like(m_ref, MASK_VALUE)
    l_ref[...] = jnp.zeros_like(l_ref)
    acc_ref[...] = jnp.zeros_like(acc_ref)

    diag = (lax.broadcasted_iota(jnp.int32, (ATT_TK, ATT_TG), 0)
            <= lax.broadcasted_iota(jnp.int32, (ATT_TK, ATT_TG), 1))

    def scores(j, g):
        koff = pl.multiple_of(j * ATT_TK, ATT_TK)
        ka = jnp.concatenate([k_ref[pl.ds(koff, ATT_TK), :], nr_ref[pl.ds(koff, ATT_TK), :]], axis=1)
        return jnp.dot(ka, qa_ref[:, g * ATT_TG:(g + 1) * ATT_TG], preferred_element_type=F32)

    def update(j, g, s, masked):
        lsl = slice(g * ATT_TG, (g + 1) * ATT_TG)
        if masked:
            s = jnp.where(diag, s, MASK_VALUE)
        d = cs_ref[h * nkb + i * ATT_GROUPS + g] - cs_ref[h * nkb + j]
        m_old = m_ref[:, lsl]
        m_new = jnp.maximum(m_old, jnp.max(s, axis=0, keepdims=True) + d)
        alpha = jnp.exp2(m_old - m_new)
        p = jnp.exp2(s - (m_new - d))
        l_ref[:, lsl] = alpha * l_ref[:, lsl] + jnp.sum(p, axis=0, keepdims=True)
        pv = jnp.dot(vt_ref[j], p.astype(BF16), preferred_element_type=F32)
        acc_ref[:, lsl] = acc_ref[:, lsl] * alpha + pv
        m_ref[:, lsl] = m_new

    groups = range(ATT_GROUPS)
    n = i * ATT_GROUPS
    for g in groups:
        s_ref[0, g] = scores(0, g)

    def body(jj, carry):
        j = 2 * jj
        for g in groups:
            s_ref[1, g] = scores(j + 1, g)
        for g in groups:
            update(j, g, s_ref[0, g], False)
        for g in groups:
            s_ref[0, g] = scores(j + 2, g)
        for g in groups:
            update(j + 1, g, s_ref[1, g], False)
        return carry

    lax.fori_loop(0, n // 2, body, 0)
    for g in groups:
        for dj in range(g + 1):
            s = s_ref[0, g] if dj == 0 else scores(n + dj, g)
            update(n + dj, g, s, dj == g)

    out = (acc_ref[...] / l_ref[...]).T
    z = z_ref[...].astype(F32)
    o_ref[...] = (out * _silu(z)).astype(BF16)


def _attn(cstart, proj, nr):
    s = proj.shape[0]
    nkb = s // ATT_TK
    nh = ATT_HEADS
    grid_spec = pltpu.PrefetchScalarGridSpec(
        num_scalar_prefetch=1,
        grid=(nh, s // ATT_TQ),
        in_specs=[
            pl.BlockSpec((ATT_TQ, HEAD_DIM), lambda h, i, cs: (i, h)),
            pl.BlockSpec((s, HEAD_DIM), lambda h, i, cs: (0, nh + h)),
            pl.BlockSpec((s, HEAD_DIM), lambda h, i, cs: (0, 2 * nh + h)),
            pl.BlockSpec((s, LANES), lambda h, i, cs: (0, 0)),
            pl.BlockSpec((ATT_TQ, HEAD_DIM), lambda h, i, cs: (i, 3 * nh + h)),
        ],
        out_specs=pl.BlockSpec((ATT_TQ, HEAD_DIM), lambda h, i, cs: (i, h)),
        scratch_shapes=[
            pltpu.VMEM((nkb, HEAD_DIM, ATT_TK), BF16),
            pltpu.VMEM((2 * HEAD_DIM, ATT_TQ), BF16),
            pltpu.VMEM((2, ATT_GROUPS, ATT_TK, ATT_TG), F32),
            pltpu.VMEM((HEAD_DIM, ATT_TQ), F32),
            pltpu.VMEM((1, ATT_TQ), F32),
            pltpu.VMEM((1, ATT_TQ), F32),
        ],
    )
    return pl.pallas_call(
        functools.partial(_attn_kernel, nkb=nkb),
        grid_spec=grid_spec,
        out_shape=jax.ShapeDtypeStruct((s, ATT_W), BF16),
        compiler_params=pltpu.CompilerParams(
            dimension_semantics=("arbitrary", "arbitrary"),
            vmem_limit_bytes=_vmem_limit(48 * 1024 * 1024),
        ),
        name="attn",
    )(cstart, proj, proj, proj, nr, proj)


def _out_kernel(h_ref, ycs_ref, ya_ref, p_ref, wo_ref, gp_ref, wpg_ref, wpp_ref, o_ref):
    ycat = jnp.concatenate([ycs_ref[...], ya_ref[...]], axis=1)
    y = jnp.dot(ycat, wo_ref[...], preferred_element_type=F32)
    inv = lax.rsqrt(jnp.mean(y * y, axis=-1, keepdims=True) + EPS)
    h1 = h_ref[...] + y * inv * gp_ref[...]
    gate = jax.nn.sigmoid(jnp.dot(h1.astype(BF16), wpg_ref[...], preferred_element_type=F32))
    pp = jnp.dot(p_ref[...].astype(BF16), wpp_ref[...], preferred_element_type=F32)
    o_ref[...] = h1 + gate * pp


def _out(h, ycs, yatt, p, wo, gp, wpg, wpp):
    s = h.shape[0]

    def const(shape):
        return pl.BlockSpec(shape, lambda i: (0, 0), pipeline_mode=pl.Buffered(1))

    return pl.pallas_call(
        _out_kernel,
        grid=(s // OUT_TM,),
        in_specs=[
            pl.BlockSpec((OUT_TM, D_MODEL), lambda i: (i, 0)),
            pl.BlockSpec((OUT_TM, CONV_W + SGU_W), lambda i: (i, 0)),
            pl.BlockSpec((OUT_TM, ATT_W), lambda i: (i, 0)),
            pl.BlockSpec((OUT_TM, PLE_DIM), lambda i: (i, 0)),
            const((D_MODEL, D_MODEL)),
            const((1, D_MODEL)),
            const((D_MODEL, D_MODEL)),
            const((PLE_DIM, D_MODEL)),
        ],
        out_specs=pl.BlockSpec((OUT_TM, D_MODEL), lambda i: (i, 0)),
        out_shape=jax.ShapeDtypeStruct((s, D_MODEL), F32),
        compiler_params=pltpu.CompilerParams(
            dimension_semantics=("arbitrary",),
            vmem_limit_bytes=_vmem_limit(52 * 1024 * 1024),
        ),
        name="out",
    )(h, ycs, yatt, p, wo, gp, wpg, wpp)


def kernel(x, p, norm_pre, w_in, b_f, conv_dw, conv_dw_b, conv_ln_g, conv_ln_b, conv_pw, conv_pw_b,
           sgu_ln_g, sgu_ln_b, sgu_w, sgu_b, w_out, norm_post, w_pg, w_pp):
    bsz, seq, _ = x.shape
    depth = w_in.shape[0]
    assert bsz == 1 and seq % PROJ_TM == 0 and seq % ATT_TQ == 0

    n_att = 4 * ATT_W
    colscale = jnp.concatenate(
        [jnp.full((1, ATT_W), HEAD_DIM ** -0.5 * LOG2E, F32), jnp.ones((1, N_MAIN - ATT_W), F32)], axis=1)
    tri = jnp.tril(jnp.ones((FG_T, FG_T), BF16))
    row = lambda v: v.reshape(1, -1)

    h = x[0]
    for i in range(depth):
        w_main = jnp.concatenate([w_in[i, :, :n_att], w_in[i, :, n_att + ATT_HEADS:]], axis=1).astype(BF16)
        w_f = jnp.pad(w_in[i, :, n_att:n_att + ATT_HEADS], ((0, 0), (0, LANES - ATT_HEADS))).astype(BF16)
        bf_pad = jnp.pad(b_f[i], (0, LANES - ATT_HEADS)).reshape(1, LANES)
        bsb = jnp.broadcast_to(sgu_b[i][:, :, None], (SGU_HEADS, CHUNK, HEAD_DIM))

        proj, flog = _proj(h, row(norm_pre[i]), w_main, w_f, colscale)
        nr, cs = _forget(flog, bf_pad, tri)
        cstart = cs[:, 0, :ATT_HEADS].T.reshape(-1)
        ycs = _branch(proj, conv_dw[i], row(conv_dw_b[i]), row(conv_ln_g[i]), row(conv_ln_b[i]),
                      conv_pw[i].astype(BF16), row(conv_pw_b[i]),
                      row(sgu_ln_g[i]), row(sgu_ln_b[i]), sgu_w[i], bsb)
        yatt = _attn(cstart, proj, nr)
        h = _out(h, ycs, yatt, p[i, 0], w_out[i].astype(BF16), row(norm_post[i]),
                 w_pg[i].astype(BF16), w_pp[i].astype(BF16))
    return h[None]
```

```python
import functools

import jax
import jax.numpy as jnp
from jax import lax
from jax.experimental import pallas as pl
from jax.experimental.pallas import tpu as pltpu

F32 = jnp.float32
BF16 = jnp.bfloat16

D_MODEL = 2048
HEAD_DIM = 128
ATT_HEADS = 8
ATT_W = ATT_HEADS * HEAD_DIM
CONV_W = 512
SGU_HEADS = 4
SGU_W = SGU_HEADS * HEAD_DIM
CONV_K = 31
CHUNK = 128
PLE_DIM = 256
EPS = 1e-6
N_MAIN = 4 * ATT_W + 3 * CONV_W + 3 * SGU_W

LANES = 128
SUBLANES = 8
VMEM_BYTES_V7X = 64 * 1024 * 1024

PROJ_TM = 1024
PROJ_TN = 1024
FG_T = 512
BR_TM = 256
CONV_HALO = 32
ATT_TK = FG_T
ATT_TG = ATT_TK
ATT_GROUPS = 2
ATT_TQ = ATT_GROUPS * ATT_TG
OUT_TM = 256
MASK_VALUE = -1e30
LOG2E = 1.4426950408889634
CS_FIRST, CS_LAST, CS_QN, CS_KN, CS_ROWS = 0, 1, 2, 3, 4
PRUNE_GAP = 160.0
PRUNE_NORM_SLACK = 1.01


def _vmem_limit(nbytes):
    assert nbytes < VMEM_BYTES_V7X
    return int(nbytes)


def _proj_kernel(x_ref, g_ref, w_ref, wf_ref, cs_ref, o_ref, f_ref, xn_ref):
    @pl.when(pl.program_id(1) == 0)
    def _():
        x = x_ref[...]
        inv = lax.rsqrt(jnp.mean(x * x, axis=-1, keepdims=True) + EPS)
        xn_ref[...] = (x * inv * g_ref[...]).astype(BF16)
        f_ref[...] = jnp.dot(xn_ref[...], wf_ref[...], preferred_element_type=F32)

    acc = jnp.dot(xn_ref[...], w_ref[...], preferred_element_type=F32)
    o_ref[...] = (acc * cs_ref[...]).astype(BF16)


def _proj(h, g, w_main, w_f, colscale):
    s = h.shape[0]
    grid = (s // PROJ_TM, N_MAIN // PROJ_TN)
    return pl.pallas_call(
        _proj_kernel,
        grid=grid,
        in_specs=[
            pl.BlockSpec((PROJ_TM, D_MODEL), lambda i, j: (i, 0)),
            pl.BlockSpec((1, D_MODEL), lambda i, j: (0, 0)),
            pl.BlockSpec((D_MODEL, PROJ_TN), lambda i, j: (0, j)),
            pl.BlockSpec((D_MODEL, LANES), lambda i, j: (0, 0)),
            pl.BlockSpec((1, PROJ_TN), lambda i, j: (0, j)),
        ],
        out_specs=[
            pl.BlockSpec((PROJ_TM, PROJ_TN), lambda i, j: (i, j)),
            pl.BlockSpec((PROJ_TM, LANES), lambda i, j: (i, 0)),
        ],
        out_shape=[
            jax.ShapeDtypeStruct((s, N_MAIN), BF16),
            jax.ShapeDtypeStruct((s, LANES), F32),
        ],
        scratch_shapes=[pltpu.VMEM((PROJ_TM, D_MODEL), BF16)],
        compiler_params=pltpu.CompilerParams(
            dimension_semantics=("arbitrary", "arbitrary"),
            vmem_limit_bytes=_vmem_limit(48 * 1024 * 1024),
        ),
        name="proj",
    )(h, g, w_main, w_f, colscale)


def _split3(x):
    p1 = x.astype(BF16)
    r1 = x - p1.astype(F32)
    p2 = r1.astype(BF16)
    p3 = (r1 - p2.astype(F32)).astype(BF16)
    return p1, p2, p3


def _max_row_norm(x_ref):
    lane = lax.broadcasted_iota(jnp.int32, (1, LANES), 1)
    out = jnp.zeros((1, LANES), F32)
    for hh in range(ATT_HEADS):
        x = x_ref[:, hh * HEAD_DIM:(hh + 1) * HEAD_DIM].astype(F32)
        ss = jnp.max(jnp.sum(x * x, axis=-1, keepdims=True), axis=0, keepdims=True)
        out = jnp.where(lane == hh, jnp.sqrt(ss), out)
    return out


def _forget_kernel(f_ref, bf_ref, tri_ref, q_ref, k_ref, nr_ref, cs_ref, carry_ref):
    @pl.when(pl.program_id(0) == 0)
    def _():
        carry_ref[...] = jnp.zeros_like(carry_ref)

    lane = lax.broadcasted_iota(jnp.int32, (FG_T, LANES), 1)
    row = lax.broadcasted_iota(jnp.int32, (FG_T, LANES), 0)
    z = f_ref[...] + bf_ref[...]
    logf = (jnp.minimum(z, 0.0) - jnp.log1p(jnp.exp(-jnp.abs(z)))) * LOG2E
    logf = jnp.where(lane < ATT_HEADS, logf, 0.0)
    first = logf[0:1, :]
    p1, p2, p3 = _split3(jnp.where(row == 0, 0.0, logf))
    tri = tri_ref[...]
    r = (jnp.dot(tri, p1, preferred_element_type=F32)
         + jnp.dot(tri, p2, preferred_element_type=F32)
         + jnp.dot(tri, p3, preferred_element_type=F32))
    c_first = carry_ref[0:1, :] + first
    c_last = c_first + r[FG_T - 1:FG_T, :]
    carry_ref[...] = jnp.broadcast_to(c_last, carry_ref.shape)
    srow = lax.broadcasted_iota(jnp.int32, (SUBLANES, LANES), 0)
    stats = jnp.where(srow == CS_FIRST, c_first,
                      jnp.where(srow == CS_LAST, c_last,
                                jnp.where(srow == CS_QN, _max_row_norm(q_ref),
                                          jnp.where(srow == CS_KN, _max_row_norm(k_ref), 0.0))))
    cs_ref[...] = stats[None]

    n1, n2, n3 = _split3(-r)
    packed = jnp.where(
        lane < ATT_HEADS, n1.astype(F32),
        jnp.where(lane < 2 * ATT_HEADS, pltpu.roll(n2.astype(F32), ATT_HEADS, 1),
                  jnp.where(lane < 3 * ATT_HEADS, pltpu.roll(n3.astype(F32), 2 * ATT_HEADS, 1), 0.0)))
    nr_ref[...] = packed.astype(BF16)


def _forget(flog, bf_pad, tri, proj):
    s = flog.shape[0]
    nblk = s // FG_T
    return pl.pallas_call(
        _forget_kernel,
        grid=(nblk,),
        in_specs=[
            pl.BlockSpec((FG_T, LANES), lambda j: (j, 0)),
            pl.BlockSpec((1, LANES), lambda j: (0, 0)),
            pl.BlockSpec((FG_T, FG_T), lambda j: (0, 0)),
            pl.BlockSpec((FG_T, ATT_W), lambda j: (j, 0)),
            pl.BlockSpec((FG_T, ATT_W), lambda j: (j, 1)),
        ],
        out_specs=[
            pl.BlockSpec((FG_T, LANES), lambda j: (j, 0)),
            pl.BlockSpec((1, SUBLANES, LANES), lambda j: (j, 0, 0)),
        ],
        out_shape=[
            jax.ShapeDtypeStruct((s, LANES), BF16),
            jax.ShapeDtypeStruct((nblk, SUBLANES, LANES), F32),
        ],
        scratch_shapes=[pltpu.VMEM((SUBLANES, LANES), F32)],
        compiler_params=pltpu.CompilerParams(dimension_semantics=("arbitrary",)),
        name="forget",
    )(flog, bf_pad, tri, proj, proj)


def _layernorm(x, g, b):
    mu = jnp.mean(x, axis=-1, keepdims=True)
    xc = x - mu
    var = jnp.mean(xc * xc, axis=-1, keepdims=True)
    return xc * lax.rsqrt(var + EPS) * g + b


def _silu(x):
    return x * jax.nn.sigmoid(x)


def _gelu(x):
    return 0.5 * x * (1.0 + lax.erf(x * (2.0 ** -0.5)))


def _branch_kernel(a_ref, b_ref, ah_ref, bh_ref, zc_ref, u_ref, v_ref, zs_ref,
                   dw_ref, dwb_ref, cg_ref, cb_ref, pw_ref, pwb_ref,
                   sg_ref, sb_ref, ws_ref, bs_ref,
                   o_ref, y_ref, c_ref):
    i = pl.program_id(0)

    ah = ah_ref[...].astype(F32)
    bh = bh_ref[...].astype(F32)
    hist = ah * jax.nn.sigmoid(bh)
    y_ref[0:CONV_HALO, :] = jnp.where(i == 0, 0.0, hist)
    a = a_ref[...].astype(F32)
    b = b_ref[...].astype(F32)
    y_ref[CONV_HALO:, :] = a * jax.nn.sigmoid(b)

    base = CONV_HALO - (CONV_K - 1)
    for lb in range(CONV_W // LANES):
        lsl = slice(lb * LANES, (lb + 1) * LANES)
        taps = [jnp.broadcast_to(dw_ref[j:j + 1, lsl], (SUBLANES, LANES)) for j in range(CONV_K)]
        bias = jnp.broadcast_to(dwb_ref[:, lsl], (SUBLANES, LANES))

        for r0 in range(0, BR_TM, SUBLANES):
            acc = bias
            for j in range(CONV_K):
                acc = acc + taps[j] * y_ref[r0 + base + j:r0 + base + j + SUBLANES, lsl]
            c_ref[r0:r0 + SUBLANES, lsl] = acc

    yc = _silu(_layernorm(c_ref[...], cg_ref[...], cb_ref[...]))
    yc = jnp.dot(yc.astype(BF16), pw_ref[...], preferred_element_type=F32) + pwb_ref[...]
    zc = zc_ref[...].astype(F32)
    o_ref[:, 0:CONV_W] = (yc * _silu(zc)).astype(BF16)

    vln = _layernorm(_gelu(v_ref[...].astype(F32)), sg_ref[...], sb_ref[...]).astype(BF16)
    tr = lax.broadcasted_iota(jnp.int32, (CHUNK, CHUNK), 0)
    tc = lax.broadcasted_iota(jnp.int32, (CHUNK, CHUNK), 1)
    for hh in range(SGU_HEADS):
        w = jnp.where(tr >= tc, ws_ref[hh], 0.0).astype(BF16)
        for c in range(BR_TM // CHUNK):
            rs = slice(c * CHUNK, (c + 1) * CHUNK)
            cs = slice(hh * HEAD_DIM, (hh + 1) * HEAD_DIM)
            sv = jnp.dot(w, vln[rs, cs], preferred_element_type=F32) + bs_ref[hh]
            c_ref[rs, cs] = sv
    gu = _gelu(u_ref[...].astype(F32))
    zs = zs_ref[...].astype(F32)
    o_ref[:, CONV_W:] = (gu * c_ref[...] * _silu(zs)).astype(BF16)


def _branch(proj, dw, dwb, cg, cb, pw, pwb, sg, sb, ws, bsb):
    s = proj.shape[0]
    w = CONV_W
    c0 = 4 * ATT_W // w
    hpb = BR_TM // CONV_HALO

    def col(k):
        return pl.BlockSpec((BR_TM, w), lambda i, k=k: (i, c0 + k))

    def halo(k):
        return pl.BlockSpec((CONV_HALO, w), lambda i, k=k: (jnp.maximum(i * hpb - 1, 0), c0 + k))

    def full(shape):
        return pl.BlockSpec(shape, lambda i: (0,) * len(shape))

    return pl.pallas_call(
        _branch_kernel,
        grid=(s // BR_TM,),
        in_specs=[
            col(0), col(1), halo(0), halo(1), col(2), col(3), col(4), col(5),
            full((CONV_K, w)), full((1, w)), full((1, w)), full((1, w)), full((w, w)), full((1, w)),
            full((1, w)), full((1, w)), full((SGU_HEADS, CHUNK, CHUNK)), full((SGU_HEADS, CHUNK, HEAD_DIM)),
        ],
        out_specs=pl.BlockSpec((BR_TM, 2 * w), lambda i: (i, 0)),
        out_shape=jax.ShapeDtypeStruct((s, 2 * w), BF16),
        scratch_shapes=[
            pltpu.VMEM((CONV_HALO + BR_TM, w), F32),
            pltpu.VMEM((BR_TM, w), F32),
        ],
        compiler_params=pltpu.CompilerParams(
            dimension_semantics=("arbitrary",),
            vmem_limit_bytes=_vmem_limit(40 * 1024 * 1024),
        ),
        name="branch",
    )(proj, proj, proj, proj, proj, proj, proj, proj, dw, dwb, cg, cb, pw, pwb, sg, sb, ws, bsb)


def _attn_kernel(cs_ref, q_ref, k_ref, v_ref, nr_ref, z_ref, o_ref,
                 vt_ref, qa_ref, s_ref, acc_ref, m_ref, l_ref, *, nkb):
    h = pl.program_id(0)
    i = pl.program_id(1)

    @pl.when(i == 0)
    def _():
        def xpose(c, carry):
            blk = v_ref[pl.ds(pl.multiple_of(c * ATT_TK, ATT_TK), ATT_TK), :]
            vt_ref[c] = blk.astype(F32).T.astype(BF16)
            return carry
        lax.fori_loop(0, nkb, xpose, 0)

    qa_ref[0:HEAD_DIM, :] = q_ref[...].astype(F32).T.astype(BF16)
    rr = lax.broadcasted_iota(jnp.int32, (LANES, ATT_TQ), 0)
    pick = (rr == h) | (rr == h + ATT_HEADS) | (rr == h + 2 * ATT_HEADS)
    qa_ref[HEAD_DIM:, :] = jnp.where(pick, 1.0, 0.0).astype(BF16)
    m_ref[...] = jnp.full_like(m_ref, MASK_VALUE)
    l_ref[...] = jnp.zeros_like(l_ref)
    acc_ref[...] = jnp.zeros_like(acc_ref)

    diag = (lax.broadcasted_iota(jnp.int32, (ATT_TK, ATT_TG), 0)
            <= lax.broadcasted_iota(jnp.int32, (ATT_TK, ATT_TG), 1))

    n = i * ATT_GROUPS

    def stat(row, j):
        return cs_ref[(row * ATT_HEADS + h) * nkb + j]

    def first_needed(g):
        qn = stat(CS_QN, n + g) * PRUNE_NORM_SLACK
        c_q = stat(CS_FIRST, n + g)
        kn_self = stat(CS_KN, n + g)

        def advance(j, jm):
            gap = qn * (stat(CS_KN, j) + kn_self) + (c_q - stat(CS_LAST, j))
            return jnp.where((gap < -PRUNE_GAP) & (jm == j), j + 1, jm)

        return lax.fori_loop(0, n, advance, 0)

    def scores(j, g):
        koff = pl.multiple_of(j * ATT_TK, ATT_TK)
        ka = jnp.concatenate([k_ref[pl.ds(koff, ATT_TK), :], nr_ref[pl.ds(koff, ATT_TK), :]], axis=1)
        return jnp.dot(ka, qa_ref[:, g * ATT_TG:(g + 1) * ATT_TG], preferred_element_type=F32)

    def update(j, g, s, masked):
        lsl = slice(g * ATT_TG, (g + 1) * ATT_TG)
        if masked:
            s = jnp.where(diag, s, MASK_VALUE)
        d = stat(CS_FIRST, n + g) - stat(CS_FIRST, j)
        m_old = m_ref[:, lsl]
        m_new = jnp.maximum(m_old, jnp.max(s, axis=0, keepdims=True) + d)
        alpha = jnp.exp2(m_old - m_new)
        p = jnp.exp2(s - (m_new - d))
        l_ref[:, lsl] = alpha * l_ref[:, lsl] + jnp.sum(p, axis=0, keepdims=True)
        pv = jnp.dot(vt_ref[j], p.astype(BF16), preferred_element_type=F32)
        acc_ref[:, lsl] = acc_ref[:, lsl] * alpha + pv
        m_ref[:, lsl] = m_new

    groups = range(ATT_GROUPS)
    j_first = first_needed(0)
    for g in groups[1:]:
        j_first = jnp.minimum(j_first, first_needed(g))
    jj0 = j_first // 2
    for g in groups:
        s_ref[0, g] = scores(2 * jj0, g)

    def body(jj, carry):
        j = 2 * jj
        for g in groups:
            s_ref[1, g] = scores(j + 1, g)
        for g in groups:
            update(j, g, s_ref[0, g], False)
        for g in groups:
            s_ref[0, g] = scores(j + 2, g)
        for g in groups:
            update(j + 1, g, s_ref[1, g], False)
        return carry

    lax.fori_loop(jj0, n // 2, body, 0)
    for g in groups:
        for dj in range(g + 1):
            s = s_ref[0, g] if dj == 0 else scores(n + dj, g)
            update(n + dj, g, s, dj == g)

    out = (acc_ref[...] / l_ref[...]).T
    z = z_ref[...].astype(F32)
    o_ref[...] = (out * _silu(z)).astype(BF16)


def _attn(cstart, proj, nr):
    s = proj.shape[0]
    nkb = s // ATT_TK
    nh = ATT_HEADS
    grid_spec = pltpu.PrefetchScalarGridSpec(
        num_scalar_prefetch=1,
        grid=(nh, s // ATT_TQ),
        in_specs=[
            pl.BlockSpec((ATT_TQ, HEAD_DIM), lambda h, i, cs: (i, h)),
            pl.BlockSpec((s, HEAD_DIM), lambda h, i, cs: (0, nh + h)),
            pl.BlockSpec((s, HEAD_DIM), lambda h, i, cs: (0, 2 * nh + h)),
            pl.BlockSpec((s, LANES), lambda h, i, cs: (0, 0)),
            pl.BlockSpec((ATT_TQ, HEAD_DIM), lambda h, i, cs: (i, 3 * nh + h)),
        ],
        out_specs=pl.BlockSpec((ATT_TQ, HEAD_DIM), lambda h, i, cs: (i, h)),
        scratch_shapes=[
            pltpu.VMEM((nkb, HEAD_DIM, ATT_TK), BF16),
            pltpu.VMEM((2 * HEAD_DIM, ATT_TQ), BF16),
            pltpu.VMEM((2, ATT_GROUPS, ATT_TK, ATT_TG), F32),
            pltpu.VMEM((HEAD_DIM, ATT_TQ), F32),
            pltpu.VMEM((1, ATT_TQ), F32),
            pltpu.VMEM((1, ATT_TQ), F32),
        ],
    )
    return pl.pallas_call(
        functools.partial(_attn_kernel, nkb=nkb),
        grid_spec=grid_spec,
        out_shape=jax.ShapeDtypeStruct((s, ATT_W), BF16),
        compiler_params=pltpu.CompilerParams(
            dimension_semantics=("arbitrary", "arbitrary"),
            vmem_limit_bytes=_vmem_limit(48 * 1024 * 1024),
        ),
        name="attn",
    )(cstart, proj, proj, proj, nr, proj)


def _out_kernel(h_ref, ycs_ref, ya_ref, p_ref, wo_ref, gp_ref, wpg_ref, wpp_ref, o_ref):
    ycat = jnp.concatenate([ycs_ref[...], ya_ref[...]], axis=1)
    y = jnp.dot(ycat, wo_ref[...], preferred_element_type=F32)
    inv = lax.rsqrt(jnp.mean(y * y, axis=-1, keepdims=True) + EPS)
    h1 = h_ref[...] + y * inv * gp_ref[...]
    gate = jax.nn.sigmoid(jnp.dot(h1.astype(BF16), wpg_ref[...], preferred_element_type=F32))
    pp = jnp.dot(p_ref[...].astype(BF16), wpp_ref[...], preferred_element_type=F32)
    o_ref[...] = h1 + gate * pp


def _out(h, ycs, yatt, p, wo, gp, wpg, wpp):
    s = h.shape[0]

    def const(shape):
        return pl.BlockSpec(shape, lambda i: (0, 0), pipeline_mode=pl.Buffered(1))

    return pl.pallas_call(
        _out_kernel,
        grid=(s // OUT_TM,),
        in_specs=[
            pl.BlockSpec((OUT_TM, D_MODEL), lambda i: (i, 0)),
            pl.BlockSpec((OUT_TM, CONV_W + SGU_W), lambda i: (i, 0)),
            pl.BlockSpec((OUT_TM, ATT_W), lambda i: (i, 0)),
            pl.BlockSpec((OUT_TM, PLE_DIM), lambda i: (i, 0)),
            const((D_MODEL, D_MODEL)),
            const((1, D_MODEL)),
            const((D_MODEL, D_MODEL)),
            const((PLE_DIM, D_MODEL)),
        ],
        out_specs=pl.BlockSpec((OUT_TM, D_MODEL), lambda i: (i, 0)),
        out_shape=jax.ShapeDtypeStruct((s, D_MODEL), F32),
        compiler_params=pltpu.CompilerParams(
            dimension_semantics=("arbitrary",),
            vmem_limit_bytes=_vmem_limit(52 * 1024 * 1024),
        ),
        name="out",
    )(h, ycs, yatt, p, wo, gp, wpg, wpp)


def kernel(x, p, norm_pre, w_in, b_f, conv_dw, conv_dw_b, conv_ln_g, conv_ln_b, conv_pw, conv_pw_b,
           sgu_ln_g, sgu_ln_b, sgu_w, sgu_b, w_out, norm_post, w_pg, w_pp):
    bsz, seq, _ = x.shape
    depth = w_in.shape[0]
    assert bsz == 1 and seq % PROJ_TM == 0 and seq % ATT_TQ == 0

    n_att = 4 * ATT_W
    colscale = jnp.concatenate(
        [jnp.full((1, ATT_W), HEAD_DIM ** -0.5 * LOG2E, F32), jnp.ones((1, N_MAIN - ATT_W), F32)], axis=1)
    tri = jnp.tril(jnp.ones((FG_T, FG_T), BF16))
    row = lambda v: v.reshape(1, -1)

    h = x[0]
    for i in range(depth):
        w_main = jnp.concatenate([w_in[i, :, :n_att], w_in[i, :, n_att + ATT_HEADS:]], axis=1).astype(BF16)
        w_f = jnp.pad(w_in[i, :, n_att:n_att + ATT_HEADS], ((0, 0), (0, LANES - ATT_HEADS))).astype(BF16)
        bf_pad = jnp.pad(b_f[i], (0, LANES - ATT_HEADS)).reshape(1, LANES)
        bsb = jnp.broadcast_to(sgu_b[i][:, :, None], (SGU_HEADS, CHUNK, HEAD_DIM))

        proj, flog = _proj(h, row(norm_pre[i]), w_main, w_f, colscale)
        nr, cs = _forget(flog, bf_pad, tri, proj)
        cstart = jnp.transpose(cs[:, :CS_ROWS, :ATT_HEADS], (1, 2, 0)).reshape(-1)
        ycs = _branch(proj, conv_dw[i], row(conv_dw_b[i]), row(conv_ln_g[i]), row(conv_ln_b[i]),
                      conv_pw[i].astype(BF16), row(conv_pw_b[i]),
                      row(sgu_ln_g[i]), row(sgu_ln_b[i]), sgu_w[i], bsb)
        yatt = _attn(cstart, proj, nr)
        h = _out(h, ycs, yatt, p[i, 0], w_out[i].astype(BF16), row(norm_post[i]),
                 w_pg[i].astype(BF16), w_pp[i].astype(BF16))
    return h[None]
```

```python
import functools

import jax
import jax.numpy as jnp
from jax import lax
from jax.experimental import pallas as pl
from jax.experimental.pallas import tpu as pltpu

F32 = jnp.float32
BF16 = jnp.bfloat16

D_MODEL = 2048
HEAD_DIM = 128
ATT_HEADS = 8
ATT_W = ATT_HEADS * HEAD_DIM
CONV_W = 512
SGU_HEADS = 4
SGU_W = SGU_HEADS * HEAD_DIM
CONV_K = 31
CHUNK = 128
PLE_DIM = 256
EPS = 1e-6
N_MAIN = 4 * ATT_W + 3 * CONV_W + 3 * SGU_W

LANES = 128
SUBLANES = 8
VMEM_BYTES_V7X = 64 * 1024 * 1024

PROJ_TM = 1024
PROJ_TN = 1024
FG_T = 512
BR_TM = 256
CONV_HALO = 32
ATT_TK = FG_T
ATT_TG = ATT_TK
ATT_GROUPS = 2
ATT_TQ = ATT_GROUPS * ATT_TG
OUT_TM = 256
MASK_VALUE = -1e30
LOG2E = 1.4426950408889634
CS_FIRST, CS_LAST, CS_QN, CS_KN, CS_ROWS = 0, 1, 2, 3, 4
PRUNE_GAP = 160.0
PRUNE_NORM_SLACK = 1.01


def _vmem_limit(nbytes):
    assert nbytes < VMEM_BYTES_V7X
    return int(nbytes)


def _proj_kernel(x_ref, g_ref, wa_ref, wb_ref, wf_ref, cs_ref, o_ref, f_ref, xn_ref, *, na):
    @pl.when(pl.program_id(1) == 0)
    def _():
        x = x_ref[...]
        inv = lax.rsqrt(jnp.mean(x * x, axis=-1, keepdims=True) + EPS)
        xn_ref[...] = (x * inv * g_ref[...]).astype(BF16)
        f_ref[...] = jnp.dot(xn_ref[...], wf_ref[...], preferred_element_type=F32)

    def emit(w_ref):
        acc = jnp.dot(xn_ref[...], w_ref[...], preferred_element_type=F32)
        o_ref[...] = (acc * cs_ref[...]).astype(BF16)

    pl.when(pl.program_id(1) < na)(lambda: emit(wa_ref))
    pl.when(pl.program_id(1) >= na)(lambda: emit(wb_ref))


def _proj(h, g, w_att, w_mix, w_f, colscale, layer):
    s = h.shape[0]
    na = w_att.shape[2] // PROJ_TN
    nb = w_mix.shape[2] // PROJ_TN
    assert (na + nb) * PROJ_TN == N_MAIN
    grid = (s // PROJ_TM, na + nb)
    return pl.pallas_call(
        functools.partial(_proj_kernel, na=na),
        grid=grid,
        in_specs=[
            pl.BlockSpec((PROJ_TM, D_MODEL), lambda i, j: (i, 0)),
            pl.BlockSpec((1, D_MODEL), lambda i, j: (0, 0)),
            pl.BlockSpec((None, D_MODEL, PROJ_TN), lambda i, j: (layer, 0, jnp.minimum(j, na - 1))),
            pl.BlockSpec((None, D_MODEL, PROJ_TN), lambda i, j: (layer, 0, jnp.maximum(j - na, 0))),
            pl.BlockSpec((None, D_MODEL, LANES), lambda i, j: (layer, 0, 0)),
            pl.BlockSpec((1, PROJ_TN), lambda i, j: (0, j)),
        ],
        out_specs=[
            pl.BlockSpec((PROJ_TM, PROJ_TN), lambda i, j: (i, j)),
            pl.BlockSpec((PROJ_TM, LANES), lambda i, j: (i, 0)),
        ],
        out_shape=[
            jax.ShapeDtypeStruct((s, N_MAIN), BF16),
            jax.ShapeDtypeStruct((s, LANES), F32),
        ],
        scratch_shapes=[pltpu.VMEM((PROJ_TM, D_MODEL), BF16)],
        compiler_params=pltpu.CompilerParams(
            dimension_semantics=("arbitrary", "arbitrary"),
            vmem_limit_bytes=_vmem_limit(56 * 1024 * 1024),
        ),
        name="proj",
    )(h, g, w_att, w_mix, w_f, colscale)


def _split3(x):
    p1 = x.astype(BF16)
    r1 = x - p1.astype(F32)
    p2 = r1.astype(BF16)
    p3 = (r1 - p2.astype(F32)).astype(BF16)
    return p1, p2, p3


def _max_row_norm(x_ref):
    lane = lax.broadcasted_iota(jnp.int32, (1, LANES), 1)
    out = jnp.zeros((1, LANES), F32)
    for hh in range(ATT_HEADS):
        x = x_ref[:, hh * HEAD_DIM:(hh + 1) * HEAD_DIM].astype(F32)
        ss = jnp.max(jnp.sum(x * x, axis=-1, keepdims=True), axis=0, keepdims=True)
        out = jnp.where(lane == hh, jnp.sqrt(ss), out)
    return out


def _forget_kernel(f_ref, bf_ref, tri_ref, q_ref, k_ref, nr_ref, cs_ref, carry_ref):
    @pl.when(pl.program_id(0) == 0)
    def _():
        carry_ref[...] = jnp.zeros_like(carry_ref)

    lane = lax.broadcasted_iota(jnp.int32, (FG_T, LANES), 1)
    row = lax.broadcasted_iota(jnp.int32, (FG_T, LANES), 0)
    z = f_ref[...] + bf_ref[...]
    logf = (jnp.minimum(z, 0.0) - jnp.log1p(jnp.exp(-jnp.abs(z)))) * LOG2E
    logf = jnp.where(lane < ATT_HEADS, logf, 0.0)
    first = logf[0:1, :]
    p1, p2, p3 = _split3(jnp.where(row == 0, 0.0, logf))
    tri = tri_ref[...]
    r = (jnp.dot(tri, p1, preferred_element_type=F32)
         + jnp.dot(tri, p2, preferred_element_type=F32)
         + jnp.dot(tri, p3, preferred_element_type=F32))
    c_first = carry_ref[0:1, :] + first
    c_last = c_first + r[FG_T - 1:FG_T, :]
    carry_ref[...] = jnp.broadcast_to(c_last, carry_ref.shape)
    srow = lax.broadcasted_iota(jnp.int32, (SUBLANES, LANES), 0)
    stats = jnp.where(srow == CS_FIRST, c_first,
                      jnp.where(srow == CS_LAST, c_last,
                                jnp.where(srow == CS_QN, _max_row_norm(q_ref),
                                          jnp.where(srow == CS_KN, _max_row_norm(k_ref), 0.0))))
    cs_ref[...] = stats[None]

    n1, n2, n3 = _split3(-r)
    packed = jnp.where(
        lane < ATT_HEADS, n1.astype(F32),
        jnp.where(lane < 2 * ATT_HEADS, pltpu.roll(n2.astype(F32), ATT_HEADS, 1),
                  jnp.where(lane < 3 * ATT_HEADS, pltpu.roll(n3.astype(F32), 2 * ATT_HEADS, 1), 0.0)))
    nr_ref[...] = packed.astype(BF16)


def _forget(flog, bf_pad, tri, proj):
    s = flog.shape[0]
    nblk = s // FG_T
    return pl.pallas_call(
        _forget_kernel,
        grid=(nblk,),
        in_specs=[
            pl.BlockSpec((FG_T, LANES), lambda j: (j, 0)),
            pl.BlockSpec((1, LANES), lambda j: (0, 0)),
            pl.BlockSpec((FG_T, FG_T), lambda j: (0, 0)),
            pl.BlockSpec((FG_T, ATT_W), lambda j: (j, 0)),
            pl.BlockSpec((FG_T, ATT_W), lambda j: (j, 1)),
        ],
        out_specs=[
            pl.BlockSpec((FG_T, LANES), lambda j: (j, 0)),
            pl.BlockSpec((1, SUBLANES, LANES), lambda j: (j, 0, 0)),
        ],
        out_shape=[
            jax.ShapeDtypeStruct((s, LANES), BF16),
            jax.ShapeDtypeStruct((nblk, SUBLANES, LANES), F32),
        ],
        scratch_shapes=[pltpu.VMEM((SUBLANES, LANES), F32)],
        compiler_params=pltpu.CompilerParams(dimension_semantics=("arbitrary",)),
        name="forget",
    )(flog, bf_pad, tri, proj, proj)


def _layernorm(x, g, b):
    mu = jnp.mean(x, axis=-1, keepdims=True)
    xc = x - mu
    var = jnp.mean(xc * xc, axis=-1, keepdims=True)
    return xc * lax.rsqrt(var + EPS) * g + b


def _silu(x):
    return x * jax.nn.sigmoid(x)


def _gelu(x):
    return 0.5 * x * (1.0 + lax.erf(x * (2.0 ** -0.5)))


def _branch_kernel(a_ref, b_ref, ah_ref, bh_ref, zc_ref, u_ref, v_ref, zs_ref,
                   dw_ref, dwb_ref, cg_ref, cb_ref, pw_ref, pwb_ref,
                   sg_ref, sb_ref, ws_ref, bs_ref,
                   o_ref, y_ref, ysh_ref, c_ref):
    i = pl.program_id(0)

    ah = ah_ref[...].astype(F32)
    bh = bh_ref[...].astype(F32)
    hist = ah * jax.nn.sigmoid(bh)
    y_ref[0:CONV_HALO, :] = jnp.where(i == 0, 0.0, hist)
    a = a_ref[...].astype(F32)
    b = b_ref[...].astype(F32)
    y_ref[CONV_HALO:, :] = a * jax.nn.sigmoid(b)

    n_rows = CONV_HALO + BR_TM
    yv = y_ref[...]
    for b in range(1, SUBLANES):
        ysh_ref[b - 1] = pltpu.roll(yv, n_rows - b, 0)

    base = CONV_HALO - (CONV_K - 1)
    assert base >= 0
    for j in range(CONV_K):
        shift, phase = divmod(base + j, SUBLANES)
        assert BR_TM + shift * SUBLANES <= n_rows - phase
    for lb in range(CONV_W // LANES):
        lsl = slice(lb * LANES, (lb + 1) * LANES)
        taps = [jnp.broadcast_to(dw_ref[j:j + 1, lsl], (SUBLANES, LANES)) for j in range(CONV_K)]
        bias = jnp.broadcast_to(dwb_ref[:, lsl], (SUBLANES, LANES))

        for r0 in range(0, BR_TM, SUBLANES):
            acc = bias
            for j in range(CONV_K):
                shift, phase = divmod(base + j, SUBLANES)
                rows = slice(r0 + shift * SUBLANES, r0 + (shift + 1) * SUBLANES)
                src = y_ref[rows, lsl] if phase == 0 else ysh_ref[phase - 1, rows, lsl]
                acc = acc + taps[j] * src
            c_ref[r0:r0 + SUBLANES, lsl] = acc

    yc = _silu(_layernorm(c_ref[...], cg_ref[...], cb_ref[...]))
    yc = jnp.dot(yc.astype(BF16), pw_ref[...], preferred_element_type=F32) + pwb_ref[...]
    zc = zc_ref[...].astype(F32)
    o_ref[:, 0:CONV_W] = (yc * _silu(zc)).astype(BF16)

    vln = _layernorm(_gelu(v_ref[...].astype(F32)), sg_ref[...], sb_ref[...]).astype(BF16)
    tr = lax.broadcasted_iota(jnp.int32, (CHUNK, CHUNK), 0)
    tc = lax.broadcasted_iota(jnp.int32, (CHUNK, CHUNK), 1)
    for hh in range(SGU_HEADS):
        w = jnp.where(tr >= tc, ws_ref[hh], 0.0).astype(BF16)
        for c in range(BR_TM // CHUNK):
            rs = slice(c * CHUNK, (c + 1) * CHUNK)
            cs = slice(hh * HEAD_DIM, (hh + 1) * HEAD_DIM)
            sv = jnp.dot(w, vln[rs, cs], preferred_element_type=F32) + bs_ref[hh]
            c_ref[rs, cs] = sv
    gu = _gelu(u_ref[...].astype(F32))
    zs = zs_ref[...].astype(F32)
    o_ref[:, CONV_W:] = (gu * c_ref[...] * _silu(zs)).astype(BF16)


def _branch(proj, dw, dwb, cg, cb, pw, pwb, sg, sb, ws, bsb):
    s = proj.shape[0]
    w = CONV_W
    c0 = 4 * ATT_W // w
    hpb = BR_TM // CONV_HALO

    def col(k):
        return pl.BlockSpec((BR_TM, w), lambda i, k=k: (i, c0 + k))

    def halo(k):
        return pl.BlockSpec((CONV_HALO, w), lambda i, k=k: (jnp.maximum(i * hpb - 1, 0), c0 + k))

    def full(shape):
        return pl.BlockSpec(shape, lambda i: (0,) * len(shape))

    return pl.pallas_call(
        _branch_kernel,
        grid=(s // BR_TM,),
        in_specs=[
            col(0), col(1), halo(0), halo(1), col(2), col(3), col(4), col(5),
            full((CONV_K, w)), full((1, w)), full((1, w)), full((1, w)), full((w, w)), full((1, w)),
            full((1, w)), full((1, w)), full((SGU_HEADS, CHUNK, CHUNK)), full((SGU_HEADS, CHUNK, HEAD_DIM)),
        ],
        out_specs=pl.BlockSpec((BR_TM, 2 * w), lambda i: (i, 0)),
        out_shape=jax.ShapeDtypeStruct((s, 2 * w), BF16),
        scratch_shapes=[
            pltpu.VMEM((CONV_HALO + BR_TM, w), F32),
            pltpu.VMEM((SUBLANES - 1, CONV_HALO + BR_TM, w), F32),
            pltpu.VMEM((BR_TM, w), F32),
        ],
        compiler_params=pltpu.CompilerParams(
            dimension_semantics=("arbitrary",),
            vmem_limit_bytes=_vmem_limit(40 * 1024 * 1024),
        ),
        name="branch",
    )(proj, proj, proj, proj, proj, proj, proj, proj, dw, dwb, cg, cb, pw, pwb, sg, sb, ws, bsb)


def _attn_kernel(cs_ref, q_ref, k_ref, v_ref, nr_ref, z_ref, o_ref,
                 vt_ref, qa_ref, s_ref, acc_ref, m_ref, l_ref, *, nkb):
    h = pl.program_id(0)
    i = pl.program_id(1)

    @pl.when(i == 0)
    def _():
        def xpose(c, carry):
            blk = v_ref[pl.ds(pl.multiple_of(c * ATT_TK, ATT_TK), ATT_TK), :]
            vt_ref[c] = blk.astype(F32).T.astype(BF16)
            return carry
        lax.fori_loop(0, nkb, xpose, 0)

    qa_ref[0:HEAD_DIM, :] = q_ref[...].astype(F32).T.astype(BF16)
    rr = lax.broadcasted_iota(jnp.int32, (LANES, ATT_TQ), 0)
    pick = (rr == h) | (rr == h + ATT_HEADS) | (rr == h + 2 * ATT_HEADS)
    qa_ref[HEAD_DIM:, :] = jnp.where(pick, 1.0, 0.0).astype(BF16)
    m_ref[...] = jnp.full_like(m_ref, MASK_VALUE)
    l_ref[...] = jnp.zeros_like(l_ref)
    acc_ref[...] = jnp.zeros_like(acc_ref)

    diag = (lax.broadcasted_iota(jnp.int32, (ATT_TK, ATT_TG), 0)
            <= lax.broadcasted_iota(jnp.int32, (ATT_TK, ATT_TG), 1))

    n = i * ATT_GROUPS

    def stat(row, j):
        return cs_ref[(row * ATT_HEADS + h) * nkb + j]

    def first_needed(g):
        qn = stat(CS_QN, n + g) * PRUNE_NORM_SLACK
        c_q = stat(CS_FIRST, n + g)
        kn_self = stat(CS_KN, n + g)

        def advance(j, jm):
            gap = qn * (stat(CS_KN, j) + kn_self) + (c_q - stat(CS_LAST, j))
            return jnp.where((gap < -PRUNE_GAP) & (jm == j), j + 1, jm)

        return lax.fori_loop(0, n, advance, 0)

    def scores(j, g):
        koff = pl.multiple_of(j * ATT_TK, ATT_TK)
        ka = jnp.concatenate([k_ref[pl.ds(koff, ATT_TK), :], nr_ref[pl.ds(koff, ATT_TK), :]], axis=1)
        return jnp.dot(ka, qa_ref[:, g * ATT_TG:(g + 1) * ATT_TG], preferred_element_type=F32)

    def update(j, g, s, masked):
        lsl = slice(g * ATT_TG, (g + 1) * ATT_TG)
        if masked:
            s = jnp.where(diag, s, MASK_VALUE)
        d = stat(CS_FIRST, n + g) - stat(CS_FIRST, j)
        m_old = m_ref[:, lsl]
        m_new = jnp.maximum(m_old, jnp.max(s, axis=0, keepdims=True) + d)
        alpha = jnp.exp2(m_old - m_new)
        p = jnp.exp2(s - (m_new - d))
        l_ref[:, lsl] = alpha * l_ref[:, lsl] + jnp.sum(p, axis=0, keepdims=True)
        pv = jnp.dot(vt_ref[j], p.astype(BF16), preferred_element_type=F32)
        acc_ref[:, lsl] = acc_ref[:, lsl] * alpha + pv
        m_ref[:, lsl] = m_new

    groups = range(ATT_GROUPS)
    j_first = first_needed(0)
    for g in groups[1:]:
        j_first = jnp.minimum(j_first, first_needed(g))
    jj0 = j_first // 2
    for g in groups:
        s_ref[0, g] = scores(2 * jj0, g)

    def body(jj, carry):
        j = 2 * jj
        for g in groups:
            s_ref[1, g] = scores(j + 1, g)
        for g in groups:
            update(j, g, s_ref[0, g], False)
        for g in groups:
            s_ref[0, g] = scores(j + 2, g)
        for g in groups:
            update(j + 1, g, s_ref[1, g], False)
        return carry

    lax.fori_loop(jj0, n // 2, body, 0)
    for g in groups:
        for dj in range(g + 1):
            s = s_ref[0, g] if dj == 0 else scores(n + dj, g)
            update(n + dj, g, s, dj == g)

    out = (acc_ref[...] / l_ref[...]).T
    z = z_ref[...].astype(F32)
    o_ref[...] = (out * _silu(z)).astype(BF16)


def _attn(cstart, proj, nr):
    s = proj.shape[0]
    nkb = s // ATT_TK
    nh = ATT_HEADS
    grid_spec = pltpu.PrefetchScalarGridSpec(
        num_scalar_prefetch=1,
        grid=(nh, s // ATT_TQ),
        in_specs=[
            pl.BlockSpec((ATT_TQ, HEAD_DIM), lambda h, i, cs: (i, h)),
            pl.BlockSpec((s, HEAD_DIM), lambda h, i, cs: (0, nh + h)),
            pl.BlockSpec((s, HEAD_DIM), lambda h, i, cs: (0, 2 * nh + h)),
            pl.BlockSpec((s, LANES), lambda h, i, cs: (0, 0)),
            pl.BlockSpec((ATT_TQ, HEAD_DIM), lambda h, i, cs: (i, 3 * nh + h)),
        ],
        out_specs=pl.BlockSpec((ATT_TQ, HEAD_DIM), lambda h, i, cs: (i, h)),
        scratch_shapes=[
            pltpu.VMEM((nkb, HEAD_DIM, ATT_TK), BF16),
            pltpu.VMEM((2 * HEAD_DIM, ATT_TQ), BF16),
            pltpu.VMEM((2, ATT_GROUPS, ATT_TK, ATT_TG), F32),
            pltpu.VMEM((HEAD_DIM, ATT_TQ), F32),
            pltpu.VMEM((1, ATT_TQ), F32),
            pltpu.VMEM((1, ATT_TQ), F32),
        ],
    )
    return pl.pallas_call(
        functools.partial(_attn_kernel, nkb=nkb),
        grid_spec=grid_spec,
        out_shape=jax.ShapeDtypeStruct((s, ATT_W), BF16),
        compiler_params=pltpu.CompilerParams(
            dimension_semantics=("arbitrary", "arbitrary"),
            vmem_limit_bytes=_vmem_limit(48 * 1024 * 1024),
        ),
        name="attn",
    )(cstart, proj, proj, proj, nr, proj)


def _out_kernel(h_ref, ycs_ref, ya_ref, p_ref, wo_ref, gp_ref, wpg_ref, wpp_ref, o_ref):
    ycat = jnp.concatenate([ycs_ref[...], ya_ref[...]], axis=1)
    y = jnp.dot(ycat, wo_ref[...], preferred_element_type=F32)
    inv = lax.rsqrt(jnp.mean(y * y, axis=-1, keepdims=True) + EPS)
    h1 = h_ref[...] + y * inv * gp_ref[...]
    gate = jax.nn.sigmoid(jnp.dot(h1.astype(BF16), wpg_ref[...], preferred_element_type=F32))
    pp = jnp.dot(p_ref[...].astype(BF16), wpp_ref[...], preferred_element_type=F32)
    o_ref[...] = h1 + gate * pp


def _out(h, ycs, yatt, p, wo, gp, wpg, wpp, layer):
    s = h.shape[0]

    def slab(rows, cols):
        return pl.BlockSpec((None, rows, cols), lambda i: (layer, 0, 0), pipeline_mode=pl.Buffered(1))

    return pl.pallas_call(
        _out_kernel,
        grid=(s // OUT_TM,),
        in_specs=[
            pl.BlockSpec((OUT_TM, D_MODEL), lambda i: (i, 0)),
            pl.BlockSpec((OUT_TM, CONV_W + SGU_W), lambda i: (i, 0)),
            pl.BlockSpec((OUT_TM, ATT_W), lambda i: (i, 0)),
            pl.BlockSpec((None, None, OUT_TM, PLE_DIM), lambda i: (layer, 0, i, 0)),
            slab(D_MODEL, D_MODEL),
            slab(1, D_MODEL),
            slab(D_MODEL, D_MODEL),
            slab(PLE_DIM, D_MODEL),
        ],
        out_specs=pl.BlockSpec((OUT_TM, D_MODEL), lambda i: (i, 0)),
        out_shape=jax.ShapeDtypeStruct((s, D_MODEL), F32),
        compiler_params=pltpu.CompilerParams(
            dimension_semantics=("arbitrary",),
            vmem_limit_bytes=_vmem_limit(52 * 1024 * 1024),
        ),
        name="out",
    )(h, ycs, yatt, p, wo, gp, wpg, wpp)


def kernel(x, p, norm_pre, w_in, b_f, conv_dw, conv_dw_b, conv_ln_g, conv_ln_b, conv_pw, conv_pw_b,
           sgu_ln_g, sgu_ln_b, sgu_w, sgu_b, w_out, norm_post, w_pg, w_pp):
    bsz, seq, _ = x.shape
    depth = w_in.shape[0]
    assert bsz == 1 and seq % PROJ_TM == 0 and seq % ATT_TQ == 0

    n_att = 4 * ATT_W
    colscale = jnp.concatenate(
        [jnp.full((1, ATT_W), HEAD_DIM ** -0.5 * LOG2E, F32), jnp.ones((1, N_MAIN - ATT_W), F32)], axis=1)
    tri = jnp.tril(jnp.ones((FG_T, FG_T), BF16))
    row = lambda v: v.reshape(1, -1)

    w_att = w_in[:, :, :n_att].astype(BF16)
    w_mix = w_in[:, :, n_att + ATT_HEADS:].astype(BF16)
    w_f = jnp.pad(w_in[:, :, n_att:n_att + ATT_HEADS], ((0, 0), (0, 0), (0, LANES - ATT_HEADS))).astype(BF16)
    w_out_b, w_pg_b, w_pp_b, conv_pw_b16 = (w.astype(BF16) for w in (w_out, w_pg, w_pp, conv_pw))
    norm_post_r = norm_post[:, None, :]

    h = x[0]
    for i in range(depth):
        bf_pad = jnp.pad(b_f[i], (0, LANES - ATT_HEADS)).reshape(1, LANES)
        bsb = jnp.broadcast_to(sgu_b[i][:, :, None], (SGU_HEADS, CHUNK, HEAD_DIM))

        proj, flog = _proj(h, row(norm_pre[i]), w_att, w_mix, w_f, colscale, i)
        nr, cs = _forget(flog, bf_pad, tri, proj)
        cstart = jnp.transpose(cs[:, :CS_ROWS, :ATT_HEADS], (1, 2, 0)).reshape(-1)
        ycs = _branch(proj, conv_dw[i], row(conv_dw_b[i]), row(conv_ln_g[i]), row(conv_ln_b[i]),
                      conv_pw_b16[i], row(conv_pw_b[i]),
                      row(sgu_ln_g[i]), row(sgu_ln_b[i]), sgu_w[i], bsb)
        yatt = _attn(cstart, proj, nr)
        h = _out(h, ycs, yatt, p, w_out_b, norm_post_r, w_pg_b, w_pp_b, i)
    return h[None]
```

```python
import functools

import jax
import jax.numpy as jnp
from jax import lax
from jax.experimental import pallas as pl
from jax.experimental.pallas import tpu as pltpu

F32 = jnp.float32
BF16 = jnp.bfloat16

D_MODEL = 2048
HEAD_DIM = 128
ATT_HEADS = 8
ATT_W = ATT_HEADS * HEAD_DIM
CONV_W = 512
SGU_HEADS = 4
SGU_W = SGU_HEADS * HEAD_DIM
CONV_K = 31
CHUNK = 128
PLE_DIM = 256
EPS = 1e-6
N_MAIN = 4 * ATT_W + 3 * CONV_W + 3 * SGU_W

LANES = 128
SUBLANES = 8
VMEM_BYTES_V7X = 64 * 1024 * 1024

PROJ_TM = 1024
PROJ_TN = 1024
FG_T = 512
BR_TM = 256
CONV_HALO = 32
ATT_TK = FG_T
ATT_TG = ATT_TK
ATT_GROUPS = 2
ATT_TQ = ATT_GROUPS * ATT_TG
OUT_TM = 256
MASK_VALUE = -1e30
LOG2E = 1.4426950408889634
CS_FIRST, CS_LAST, CS_QN, CS_KN, CS_ROWS = 0, 1, 2, 3, 4
PRUNE_GAP = 160.0
PRUNE_NORM_SLACK = 1.01


def _vmem_limit(nbytes):
    assert nbytes < VMEM_BYTES_V7X
    return int(nbytes)


def _proj_kernel(x_ref, g_ref, wa_ref, wb_ref, wf_ref, cs_ref, o_ref, f_ref, xn_ref, *, na):
    @pl.when(pl.program_id(1) == 0)
    def _():
        x = x_ref[...]
        inv = lax.rsqrt(jnp.mean(x * x, axis=-1, keepdims=True) + EPS)
        xn_ref[...] = (x * inv * g_ref[...]).astype(BF16)
        f_ref[...] = jnp.dot(xn_ref[...], wf_ref[...], preferred_element_type=F32)

    def emit(w_ref):
        acc = jnp.dot(xn_ref[...], w_ref[...], preferred_element_type=F32)
        o_ref[...] = (acc * cs_ref[...]).astype(BF16)

    pl.when(pl.program_id(1) < na)(lambda: emit(wa_ref))
    pl.when(pl.program_id(1) >= na)(lambda: emit(wb_ref))


def _proj(h, g, w_att, w_mix, w_f, colscale, layer):
    s = h.shape[0]
    na = w_att.shape[2] // PROJ_TN
    nb = w_mix.shape[2] // PROJ_TN
    assert (na + nb) * PROJ_TN == N_MAIN
    grid = (s // PROJ_TM, na + nb)
    return pl.pallas_call(
        functools.partial(_proj_kernel, na=na),
        grid=grid,
        in_specs=[
            pl.BlockSpec((PROJ_TM, D_MODEL), lambda i, j: (i, 0)),
            pl.BlockSpec((1, D_MODEL), lambda i, j: (0, 0)),
            pl.BlockSpec((None, D_MODEL, PROJ_TN), lambda i, j: (layer, 0, jnp.minimum(j, na - 1))),
            pl.BlockSpec((None, D_MODEL, PROJ_TN),
                         lambda i, j: (layer, 0, jnp.where(j < na - 1, nb - 1, jnp.maximum(j - na, 0)))),
            pl.BlockSpec((None, D_MODEL, LANES), lambda i, j: (layer, 0, 0)),
            pl.BlockSpec((1, PROJ_TN), lambda i, j: (0, j)),
        ],
        out_specs=[
            pl.BlockSpec((PROJ_TM, PROJ_TN), lambda i, j: (i, j)),
            pl.BlockSpec((PROJ_TM, LANES), lambda i, j: (i, 0)),
        ],
        out_shape=[
            jax.ShapeDtypeStruct((s, N_MAIN), BF16),
            jax.ShapeDtypeStruct((s, LANES), F32),
        ],
        scratch_shapes=[pltpu.VMEM((PROJ_TM, D_MODEL), BF16)],
        compiler_params=pltpu.CompilerParams(
            dimension_semantics=("arbitrary", "arbitrary"),
            vmem_limit_bytes=_vmem_limit(56 * 1024 * 1024),
        ),
        name="proj",
    )(h, g, w_att, w_mix, w_f, colscale)


def _split3(x):
    p1 = x.astype(BF16)
    r1 = x - p1.astype(F32)
    p2 = r1.astype(BF16)
    p3 = (r1 - p2.astype(F32)).astype(BF16)
    return p1, p2, p3


def _max_row_norm(x_ref):
    lane = lax.broadcasted_iota(jnp.int32, (1, LANES), 1)
    out = jnp.zeros((1, LANES), F32)
    for hh in range(ATT_HEADS):
        x = x_ref[:, hh * HEAD_DIM:(hh + 1) * HEAD_DIM].astype(F32)
        ss = jnp.max(jnp.sum(x * x, axis=-1, keepdims=True), axis=0, keepdims=True)
        out = jnp.where(lane == hh, jnp.sqrt(ss), out)
    return out


def _forget_kernel(f_ref, bf_ref, tri_ref, q_ref, k_ref, nr_ref, cs_ref, jf_ref, carry_ref, st_ref, *, nblk):
    step = pl.program_id(0)

    @pl.when(step == 0)
    def _():
        carry_ref[...] = jnp.zeros_like(carry_ref)

    lane = lax.broadcasted_iota(jnp.int32, (FG_T, LANES), 1)
    row = lax.broadcasted_iota(jnp.int32, (FG_T, LANES), 0)
    z = f_ref[...] + bf_ref[...]
    logf = (jnp.minimum(z, 0.0) - jnp.log1p(jnp.exp(-jnp.abs(z)))) * LOG2E
    logf = jnp.where(lane < ATT_HEADS, logf, 0.0)
    first = logf[0:1, :]
    p1, p2, p3 = _split3(jnp.where(row == 0, 0.0, logf))
    tri = tri_ref[...]
    r = (jnp.dot(tri, p1, preferred_element_type=F32)
         + jnp.dot(tri, p2, preferred_element_type=F32)
         + jnp.dot(tri, p3, preferred_element_type=F32))
    c_first = carry_ref[0:1, :] + first
    c_last = c_first + r[FG_T - 1:FG_T, :]
    carry_ref[...] = jnp.broadcast_to(c_last, carry_ref.shape)
    srow = lax.broadcasted_iota(jnp.int32, (SUBLANES, LANES), 0)
    stats = jnp.where(srow == CS_FIRST, c_first,
                      jnp.where(srow == CS_LAST, c_last,
                                jnp.where(srow == CS_QN, _max_row_norm(q_ref),
                                          jnp.where(srow == CS_KN, _max_row_norm(k_ref), 0.0))))
    cs_ref[...] = stats[None]
    st_ref[step] = stats

    @pl.when(step == nblk - 1)
    def _():
        for gb in range(nblk):
            sg = st_ref[gb]
            qn = sg[CS_QN:CS_QN + 1] * PRUNE_NORM_SLACK
            c_q = sg[CS_FIRST:CS_FIRST + 1]
            kn_self = sg[CS_KN:CS_KN + 1]
            jm = jnp.zeros((1, LANES), jnp.int32)
            for jb in range(gb):
                sj = st_ref[jb]
                gap = qn * (sj[CS_KN:CS_KN + 1] + kn_self) + (c_q - sj[CS_LAST:CS_LAST + 1])
                jm = jnp.where((gap < -PRUNE_GAP) & (jm == jb), jb + 1, jm)
            jf_ref[gb:gb + 1, :] = jm

    n1, n2, n3 = _split3(-r)
    packed = jnp.where(
        lane < ATT_HEADS, n1.astype(F32),
        jnp.where(lane < 2 * ATT_HEADS, pltpu.roll(n2.astype(F32), ATT_HEADS, 1),
                  jnp.where(lane < 3 * ATT_HEADS, pltpu.roll(n3.astype(F32), 2 * ATT_HEADS, 1), 0.0)))
    nr_ref[...] = packed.astype(BF16)


def _forget(flog, bf_pad, tri, proj):
    s = flog.shape[0]
    nblk = s // FG_T
    return pl.pallas_call(
        functools.partial(_forget_kernel, nblk=nblk),
        grid=(nblk,),
        in_specs=[
            pl.BlockSpec((FG_T, LANES), lambda j: (j, 0)),
            pl.BlockSpec((1, LANES), lambda j: (0, 0)),
            pl.BlockSpec((FG_T, FG_T), lambda j: (0, 0)),
            pl.BlockSpec((FG_T, ATT_W), lambda j: (j, 0)),
            pl.BlockSpec((FG_T, ATT_W), lambda j: (j, 1)),
        ],
        out_specs=[
            pl.BlockSpec((FG_T, LANES), lambda j: (j, 0)),
            pl.BlockSpec((1, SUBLANES, LANES), lambda j: (j, 0, 0)),
            pl.BlockSpec((nblk, LANES), lambda j: (0, 0)),
        ],
        out_shape=[
            jax.ShapeDtypeStruct((s, LANES), BF16),
            jax.ShapeDtypeStruct((nblk, SUBLANES, LANES), F32),
            jax.ShapeDtypeStruct((nblk, LANES), jnp.int32),
        ],
        scratch_shapes=[
            pltpu.VMEM((SUBLANES, LANES), F32),
            pltpu.VMEM((nblk, SUBLANES, LANES), F32),
        ],
        compiler_params=pltpu.CompilerParams(dimension_semantics=("arbitrary",)),
        name="forget",
    )(flog, bf_pad, tri, proj, proj)


def _layernorm(x, g, b):
    mu = jnp.mean(x, axis=-1, keepdims=True)
    xc = x - mu
    var = jnp.mean(xc * xc, axis=-1, keepdims=True)
    return xc * lax.rsqrt(var + EPS) * g + b


def _silu(x):
    return x * jax.nn.sigmoid(x)


def _gelu(x):
    return 0.5 * x * (1.0 + lax.erf(x * (2.0 ** -0.5)))


def _branch_kernel(a_ref, b_ref, ah_ref, bh_ref, zc_ref, u_ref, v_ref, zs_ref,
                   dw_ref, dwb_ref, cg_ref, cb_ref, pw_ref, pwb_ref,
                   sg_ref, sb_ref, ws_ref, bs_ref,
                   o_ref, y_ref, ysh_ref, c_ref):
    i = pl.program_id(0)

    ah = ah_ref[...].astype(F32)
    bh = bh_ref[...].astype(F32)
    hist = ah * jax.nn.sigmoid(bh)
    y_ref[0:CONV_HALO, :] = jnp.where(i == 0, 0.0, hist)
    a = a_ref[...].astype(F32)
    b = b_ref[...].astype(F32)
    y_ref[CONV_HALO:, :] = a * jax.nn.sigmoid(b)

    n_rows = CONV_HALO + BR_TM
    yv = y_ref[...]
    for b in range(1, SUBLANES):
        ysh_ref[b - 1] = pltpu.roll(yv, n_rows - b, 0)

    base = CONV_HALO - (CONV_K - 1)
    assert base >= 0
    for j in range(CONV_K):
        shift, phase = divmod(base + j, SUBLANES)
        assert BR_TM + shift * SUBLANES <= n_rows - phase
    for lb in range(CONV_W // LANES):
        lsl = slice(lb * LANES, (lb + 1) * LANES)
        taps = [jnp.broadcast_to(dw_ref[j:j + 1, lsl], (SUBLANES, LANES)) for j in range(CONV_K)]
        bias = jnp.broadcast_to(dwb_ref[:, lsl], (SUBLANES, LANES))

        for r0 in range(0, BR_TM, SUBLANES):
            acc = bias
            for j in range(CONV_K):
                shift, phase = divmod(base + j, SUBLANES)
                rows = slice(r0 + shift * SUBLANES, r0 + (shift + 1) * SUBLANES)
                src = y_ref[rows, lsl] if phase == 0 else ysh_ref[phase - 1, rows, lsl]
                acc = acc + taps[j] * src
            c_ref[r0:r0 + SUBLANES, lsl] = acc

    yc = _silu(_layernorm(c_ref[...], cg_ref[...], cb_ref[...]))
    yc = jnp.dot(yc.astype(BF16), pw_ref[...], preferred_element_type=F32) + pwb_ref[...]
    zc = zc_ref[...].astype(F32)
    o_ref[:, 0:CONV_W] = (yc * _silu(zc)).astype(BF16)

    vln = _layernorm(_gelu(v_ref[...].astype(F32)), sg_ref[...], sb_ref[...]).astype(BF16)
    tr = lax.broadcasted_iota(jnp.int32, (CHUNK, CHUNK), 0)
    tc = lax.broadcasted_iota(jnp.int32, (CHUNK, CHUNK), 1)
    for hh in range(SGU_HEADS):
        w = jnp.where(tr >= tc, ws_ref[hh], 0.0).astype(BF16)
        for c in range(BR_TM // CHUNK):
            rs = slice(c * CHUNK, (c + 1) * CHUNK)
            cs = slice(hh * HEAD_DIM, (hh + 1) * HEAD_DIM)
            sv = jnp.dot(w, vln[rs, cs], preferred_element_type=F32) + bs_ref[hh]
            c_ref[rs, cs] = sv
    gu = _gelu(u_ref[...].astype(F32))
    zs = zs_ref[...].astype(F32)
    o_ref[:, CONV_W:] = (gu * c_ref[...] * _silu(zs)).astype(BF16)


def _branch(proj, dw, dwb, cg, cb, pw, pwb, sg, sb, ws, bsb):
    s = proj.shape[0]
    w = CONV_W
    c0 = 4 * ATT_W // w
    hpb = BR_TM // CONV_HALO

    def col(k):
        return pl.BlockSpec((BR_TM, w), lambda i, k=k: (i, c0 + k))

    def halo(k):
        return pl.BlockSpec((CONV_HALO, w), lambda i, k=k: (jnp.maximum(i * hpb - 1, 0), c0 + k))

    def full(shape):
        return pl.BlockSpec(shape, lambda i: (0,) * len(shape))

    return pl.pallas_call(
        _branch_kernel,
        grid=(s // BR_TM,),
        in_specs=[
            col(0), col(1), halo(0), halo(1), col(2), col(3), col(4), col(5),
            full((CONV_K, w)), full((1, w)), full((1, w)), full((1, w)), full((w, w)), full((1, w)),
            full((1, w)), full((1, w)), full((SGU_HEADS, CHUNK, CHUNK)), full((SGU_HEADS, CHUNK, HEAD_DIM)),
        ],
        out_specs=pl.BlockSpec((BR_TM, 2 * w), lambda i: (i, 0)),
        out_shape=jax.ShapeDtypeStruct((s, 2 * w), BF16),
        scratch_shapes=[
            pltpu.VMEM((CONV_HALO + BR_TM, w), F32),
            pltpu.VMEM((SUBLANES - 1, CONV_HALO + BR_TM, w), F32),
            pltpu.VMEM((BR_TM, w), F32),
        ],
        compiler_params=pltpu.CompilerParams(
            dimension_semantics=("arbitrary",),
            vmem_limit_bytes=_vmem_limit(40 * 1024 * 1024),
        ),
        name="branch",
    )(proj, proj, proj, proj, proj, proj, proj, proj, dw, dwb, cg, cb, pw, pwb, sg, sb, ws, bsb)


def _attn_kernel(cs_ref, jf_ref, q_ref, k_ref, v_ref, nr_ref, z_ref, o_ref,
                 vt_ref, qa_ref, s_ref, acc_ref, m_ref, l_ref, *, nkb):
    h = pl.program_id(0)
    i = pl.program_id(1)

    @pl.when(i == 0)
    def _():
        def xpose(c, carry):
            blk = v_ref[pl.ds(pl.multiple_of(c * ATT_TK, ATT_TK), ATT_TK), :]
            vt_ref[c] = blk.astype(F32).T.astype(BF16)
            return carry
        lax.fori_loop(0, nkb, xpose, 0)

    qa_ref[0:HEAD_DIM, :] = q_ref[...].astype(F32).T.astype(BF16)
    rr = lax.broadcasted_iota(jnp.int32, (LANES, ATT_TQ), 0)
    pick = (rr == h) | (rr == h + ATT_HEADS) | (rr == h + 2 * ATT_HEADS)
    qa_ref[HEAD_DIM:, :] = jnp.where(pick, 1.0, 0.0).astype(BF16)
    m_ref[...] = jnp.full_like(m_ref, MASK_VALUE)
    l_ref[...] = jnp.zeros_like(l_ref)
    acc_ref[...] = jnp.zeros_like(acc_ref)

    diag = (lax.broadcasted_iota(jnp.int32, (ATT_TK, ATT_TG), 0)
            <= lax.broadcasted_iota(jnp.int32, (ATT_TK, ATT_TG), 1))

    n = i * ATT_GROUPS

    def stat(row, j):
        return cs_ref[(row * ATT_HEADS + h) * nkb + j]

    def scores(j, g):
        koff = pl.multiple_of(j * ATT_TK, ATT_TK)
        ka = jnp.concatenate([k_ref[pl.ds(koff, ATT_TK), :], nr_ref[pl.ds(koff, ATT_TK), :]], axis=1)
        return jnp.dot(ka, qa_ref[:, g * ATT_TG:(g + 1) * ATT_TG], preferred_element_type=F32)

    def update(g, blocks):
        lsl = slice(g * ATT_TG, (g + 1) * ATT_TG)
        c_q = stat(CS_FIRST, n + g)
        m_old = m_ref[:, lsl]
        m_new = m_old
        shifted = []
        for j, s, masked in blocks:
            if masked:
                s = jnp.where(diag, s, MASK_VALUE)
            d = c_q - stat(CS_FIRST, j)
            m_new = jnp.maximum(m_new, jnp.max(s, axis=0, keepdims=True) + d)
            shifted.append((j, s, d))
        alpha = jnp.exp2(m_old - m_new)
        l_new = alpha * l_ref[:, lsl]
        acc = acc_ref[:, lsl] * alpha
        for j, s, d in shifted:
            p = jnp.exp2(s - (m_new - d))
            l_new = l_new + jnp.sum(p, axis=0, keepdims=True)
            acc = acc + jnp.dot(vt_ref[j], p.astype(BF16), preferred_element_type=F32)
        l_ref[:, lsl] = l_new
        acc_ref[:, lsl] = acc
        m_ref[:, lsl] = m_new

    groups = range(ATT_GROUPS)
    j_first = jf_ref[h * nkb + n]
    for g in groups[1:]:
        j_first = jnp.minimum(j_first, jf_ref[h * nkb + n + g])
    jj0 = j_first // 2
    for g in groups:
        s_ref[0, g] = scores(2 * jj0, g)

    def body(jj, carry):
        j = 2 * jj
        for g in groups:
            s_ref[1, g] = scores(j + 1, g)
        for g in groups:
            update(g, [(j, s_ref[0, g], False)])
        for g in groups:
            s_ref[0, g] = scores(j + 2, g)
        for g in groups:
            update(g, [(j + 1, s_ref[1, g], False)])
        return carry

    lax.fori_loop(jj0, n // 2, body, 0)
    for g in groups:
        update(g, [(n + dj, s_ref[0, g] if dj == 0 else scores(n + dj, g), dj == g) for dj in range(g + 1)])

    out = (acc_ref[...] / l_ref[...]).T
    z = z_ref[...].astype(F32)
    o_ref[...] = (out * _silu(z)).astype(BF16)


def _attn(cstart, jfirst, proj, nr):
    s = proj.shape[0]
    nkb = s // ATT_TK
    nh = ATT_HEADS
    grid_spec = pltpu.PrefetchScalarGridSpec(
        num_scalar_prefetch=2,
        grid=(nh, s // ATT_TQ),
        in_specs=[
            pl.BlockSpec((ATT_TQ, HEAD_DIM), lambda h, i, cs, jf: (i, h)),
            pl.BlockSpec((s, HEAD_DIM), lambda h, i, cs, jf: (0, nh + h)),
            pl.BlockSpec((s, HEAD_DIM), lambda h, i, cs, jf: (0, 2 * nh + h)),
            pl.BlockSpec((s, LANES), lambda h, i, cs, jf: (0, 0)),
            pl.BlockSpec((ATT_TQ, HEAD_DIM), lambda h, i, cs, jf: (i, 3 * nh + h)),
        ],
        out_specs=pl.BlockSpec((ATT_TQ, HEAD_DIM), lambda h, i, cs, jf: (i, h)),
        scratch_shapes=[
            pltpu.VMEM((nkb, HEAD_DIM, ATT_TK), BF16),
            pltpu.VMEM((2 * HEAD_DIM, ATT_TQ), BF16),
            pltpu.VMEM((2, ATT_GROUPS, ATT_TK, ATT_TG), F32),
            pltpu.VMEM((HEAD_DIM, ATT_TQ), F32),
            pltpu.VMEM((1, ATT_TQ), F32),
            pltpu.VMEM((1, ATT_TQ), F32),
        ],
    )
    return pl.pallas_call(
        functools.partial(_attn_kernel, nkb=nkb),
        grid_spec=grid_spec,
        out_shape=jax.ShapeDtypeStruct((s, ATT_W), BF16),
        compiler_params=pltpu.CompilerParams(
            dimension_semantics=("arbitrary", "arbitrary"),
            vmem_limit_bytes=_vmem_limit(48 * 1024 * 1024),
        ),
        name="attn",
    )(cstart, jfirst, proj, proj, proj, nr, proj)


def _out_kernel(h_ref, ycs_ref, ya_ref, p_ref, wo_ref, gp_ref, wpg_ref, wpp_ref, o_ref):
    ycat = jnp.concatenate([ycs_ref[...], ya_ref[...]], axis=1)
    y = jnp.dot(ycat, wo_ref[...], preferred_element_type=F32)
    inv = lax.rsqrt(jnp.mean(y * y, axis=-1, keepdims=True) + EPS)
    h1 = h_ref[...] + y * inv * gp_ref[...]
    gate = jax.nn.sigmoid(jnp.dot(h1.astype(BF16), wpg_ref[...], preferred_element_type=F32))
    pp = jnp.dot(p_ref[...].astype(BF16), wpp_ref[...], preferred_element_type=F32)
    o_ref[...] = h1 + gate * pp


def _out(h, ycs, yatt, p, wo, gp, wpg, wpp, layer):
    s = h.shape[0]

    def slab(rows, cols):
        return pl.BlockSpec((None, rows, cols), lambda i: (layer, 0, 0), pipeline_mode=pl.Buffered(1))

    return pl.pallas_call(
        _out_kernel,
        grid=(s // OUT_TM,),
        in_specs=[
            pl.BlockSpec((OUT_TM, D_MODEL), lambda i: (i, 0)),
            pl.BlockSpec((OUT_TM, CONV_W + SGU_W), lambda i: (i, 0)),
            pl.BlockSpec((OUT_TM, ATT_W), lambda i: (i, 0)),
            pl.BlockSpec((None, None, OUT_TM, PLE_DIM), lambda i: (layer, 0, i, 0)),
            slab(D_MODEL, D_MODEL),
            slab(1, D_MODEL),
            slab(D_MODEL, D_MODEL),
            slab(PLE_DIM, D_MODEL),
        ],
        out_specs=pl.BlockSpec((OUT_TM, D_MODEL), lambda i: (i, 0)),
        out_shape=jax.ShapeDtypeStruct((s, D_MODEL), F32),
        compiler_params=pltpu.CompilerParams(
            dimension_semantics=("arbitrary",),
            vmem_limit_bytes=_vmem_limit(52 * 1024 * 1024),
        ),
        name="out",
    )(h, ycs, yatt, p, wo, gp, wpg, wpp)


def kernel(x, p, norm_pre, w_in, b_f, conv_dw, conv_dw_b, conv_ln_g, conv_ln_b, conv_pw, conv_pw_b,
           sgu_ln_g, sgu_ln_b, sgu_w, sgu_b, w_out, norm_post, w_pg, w_pp):
    bsz, seq, _ = x.shape
    depth = w_in.shape[0]
    assert bsz == 1 and seq % PROJ_TM == 0 and seq % ATT_TQ == 0

    n_att = 4 * ATT_W
    colscale = jnp.concatenate(
        [jnp.full((1, ATT_W), HEAD_DIM ** -0.5 * LOG2E, F32), jnp.ones((1, N_MAIN - ATT_W), F32)], axis=1)
    tri = jnp.tril(jnp.ones((FG_T, FG_T), BF16))
    row = lambda v: v.reshape(1, -1)

    w_att = w_in[:, :, :n_att].astype(BF16)
    w_mix = w_in[:, :, n_att + ATT_HEADS:].astype(BF16)
    w_f = jnp.pad(w_in[:, :, n_att:n_att + ATT_HEADS], ((0, 0), (0, 0), (0, LANES - ATT_HEADS))).astype(BF16)
    w_out_b, w_pg_b, w_pp_b, conv_pw_b16 = (w.astype(BF16) for w in (w_out, w_pg, w_pp, conv_pw))
    norm_post_r = norm_post[:, None, :]

    h = x[0]
    for i in range(depth):
        bf_pad = jnp.pad(b_f[i], (0, LANES - ATT_HEADS)).reshape(1, LANES)
        bsb = jnp.broadcast_to(sgu_b[i][:, :, None], (SGU_HEADS, CHUNK, HEAD_DIM))

        proj, flog = _proj(h, row(norm_pre[i]), w_att, w_mix, w_f, colscale, i)
        nr, cs, jf = _forget(flog, bf_pad, tri, proj)
        cstart = jnp.transpose(cs[:, :CS_ROWS, :ATT_HEADS], (1, 2, 0)).reshape(-1)
        jfirst = jf[:, :ATT_HEADS].T.reshape(-1)
        ycs = _branch(proj, conv_dw[i], row(conv_dw_b[i]), row(conv_ln_g[i]), row(conv_ln_b[i]),
                      conv_pw_b16[i], row(conv_pw_b[i]),
                      row(sgu_ln_g[i]), row(sgu_ln_b[i]), sgu_w[i], bsb)
        yatt = _attn(cstart, jfirst, proj, nr)
        h = _out(h, ycs, yatt, p, w_out_b, norm_post_r, w_pg_b, w_pp_b, i)
    return h[None]
```

```python
import functools

import jax
import jax.numpy as jnp
from jax import lax
from jax.experimental import pallas as pl
from jax.experimental.pallas import tpu as pltpu

F32 = jnp.float32
BF16 = jnp.bfloat16

D_MODEL = 2048
HEAD_DIM = 128
ATT_HEADS = 8
ATT_W = ATT_HEADS * HEAD_DIM
CONV_W = 512
SGU_HEADS = 4
SGU_W = SGU_HEADS * HEAD_DIM
CONV_K = 31
CHUNK = 128
PLE_DIM = 256
EPS = 1e-6
N_MAIN = 4 * ATT_W + 3 * CONV_W + 3 * SGU_W

LANES = 128
SUBLANES = 8
VMEM_BYTES_V7X = 64 * 1024 * 1024

PROJ_TM = 1024
PROJ_TN = 1024
FG_T = 512
BR_TM = 256
CONV_HALO = 32
ATT_TK = FG_T
ATT_TG = ATT_TK
ATT_GROUPS = 2
ATT_TQ = ATT_GROUPS * ATT_TG
OUT_TM = 256
MASK_VALUE = -1e30
LOG2E = 1.4426950408889634
CS_FIRST, CS_LAST, CS_QN, CS_KN, CS_ROWS = 0, 1, 2, 3, 4
PRUNE_GAP = 160.0
PRUNE_NORM_SLACK = 1.01


def _vmem_limit(nbytes):
    assert nbytes < VMEM_BYTES_V7X
    return int(nbytes)


def _proj_kernel(x_ref, g_ref, wa_ref, wb_ref, wf_ref, cs_ref, o_ref, f_ref, xn_ref, *, na):
    @pl.when(pl.program_id(1) == 0)
    def _():
        x = x_ref[...]
        inv = lax.rsqrt(jnp.mean(x * x, axis=-1, keepdims=True) + EPS)
        xn_ref[...] = (x * inv * g_ref[...]).astype(BF16)
        f_ref[...] = jnp.dot(xn_ref[...], wf_ref[...], preferred_element_type=F32)

    def emit(w_ref):
        acc = jnp.dot(xn_ref[...], w_ref[...], preferred_element_type=F32)
        o_ref[...] = (acc * cs_ref[...]).astype(BF16)

    pl.when(pl.program_id(1) < na)(lambda: emit(wa_ref))
    pl.when(pl.program_id(1) >= na)(lambda: emit(wb_ref))


def _proj(h, g, w_att, w_mix, w_f, colscale, layer):
    s = h.shape[0]
    na = w_att.shape[2] // PROJ_TN
    nb = w_mix.shape[2] // PROJ_TN
    assert (na + nb) * PROJ_TN == N_MAIN
    grid = (s // PROJ_TM, na + nb)
    return pl.pallas_call(
        functools.partial(_proj_kernel, na=na),
        grid=grid,
        in_specs=[
            pl.BlockSpec((PROJ_TM, D_MODEL), lambda i, j: (i, 0)),
            pl.BlockSpec((1, D_MODEL), lambda i, j: (0, 0)),
            pl.BlockSpec((None, D_MODEL, PROJ_TN), lambda i, j: (layer, 0, jnp.minimum(j, na - 1))),
            pl.BlockSpec((None, D_MODEL, PROJ_TN),
                         lambda i, j: (layer, 0, jnp.where(j < na - 1, nb - 1, jnp.maximum(j - na, 0)))),
            pl.BlockSpec((None, D_MODEL, LANES), lambda i, j: (layer, 0, 0)),
            pl.BlockSpec((1, PROJ_TN), lambda i, j: (0, j)),
        ],
        out_specs=[
            pl.BlockSpec((PROJ_TM, PROJ_TN), lambda i, j: (i, j)),
            pl.BlockSpec((PROJ_TM, LANES), lambda i, j: (i, 0)),
        ],
        out_shape=[
            jax.ShapeDtypeStruct((s, N_MAIN), BF16),
            jax.ShapeDtypeStruct((s, LANES), F32),
        ],
        scratch_shapes=[pltpu.VMEM((PROJ_TM, D_MODEL), BF16)],
        compiler_params=pltpu.CompilerParams(
            dimension_semantics=("arbitrary", "arbitrary"),
            vmem_limit_bytes=_vmem_limit(56 * 1024 * 1024),
        ),
        name="proj",
    )(h, g, w_att, w_mix, w_f, colscale)


def _split3(x):
    p1 = x.astype(BF16)
    r1 = x - p1.astype(F32)
    p2 = r1.astype(BF16)
    p3 = (r1 - p2.astype(F32)).astype(BF16)
    return p1, p2, p3


def _max_row_norm(x_ref):
    lane = lax.broadcasted_iota(jnp.int32, (1, LANES), 1)
    out = jnp.zeros((1, LANES), F32)
    for hh in range(ATT_HEADS):
        x = x_ref[:, hh * HEAD_DIM:(hh + 1) * HEAD_DIM].astype(F32)
        ss = jnp.max(jnp.sum(x * x, axis=-1, keepdims=True), axis=0, keepdims=True)
        out = jnp.where(lane == hh, jnp.sqrt(ss), out)
    return out


def _forget_kernel(f_ref, bf_ref, tri_ref, q_ref, k_ref, nr_ref, cs_ref, jf_ref, carry_ref, st_ref, *, nblk):
    step = pl.program_id(0)

    @pl.when(step == 0)
    def _():
        carry_ref[...] = jnp.zeros_like(carry_ref)

    lane = lax.broadcasted_iota(jnp.int32, (FG_T, LANES), 1)
    row = lax.broadcasted_iota(jnp.int32, (FG_T, LANES), 0)
    z = f_ref[...] + bf_ref[...]
    logf = (jnp.minimum(z, 0.0) - jnp.log1p(jnp.exp(-jnp.abs(z)))) * LOG2E
    logf = jnp.where(lane < ATT_HEADS, logf, 0.0)
    first = logf[0:1, :]
    p1, p2, p3 = _split3(jnp.where(row == 0, 0.0, logf))
    tri = tri_ref[...]
    r = (jnp.dot(tri, p1, preferred_element_type=F32)
         + jnp.dot(tri, p2, preferred_element_type=F32)
         + jnp.dot(tri, p3, preferred_element_type=F32))
    c_first = carry_ref[0:1, :] + first
    c_last = c_first + r[FG_T - 1:FG_T, :]
    carry_ref[...] = jnp.broadcast_to(c_last, carry_ref.shape)
    srow = lax.broadcasted_iota(jnp.int32, (SUBLANES, LANES), 0)
    stats = jnp.where(srow == CS_FIRST, c_first,
                      jnp.where(srow == CS_LAST, c_last,
                                jnp.where(srow == CS_QN, _max_row_norm(q_ref),
                                          jnp.where(srow == CS_KN, _max_row_norm(k_ref), 0.0))))
    cs_ref[...] = stats[None]
    st_ref[step] = stats

    @pl.when(step == nblk - 1)
    def _():
        for gb in range(nblk):
            sg = st_ref[gb]
            qn = sg[CS_QN:CS_QN + 1] * PRUNE_NORM_SLACK
            c_q = sg[CS_FIRST:CS_FIRST + 1]
            kn_self = sg[CS_KN:CS_KN + 1]
            jm = jnp.zeros((1, LANES), jnp.int32)
            for jb in range(gb):
                sj = st_ref[jb]
                gap = qn * (sj[CS_KN:CS_KN + 1] + kn_self) + (c_q - sj[CS_LAST:CS_LAST + 1])
                jm = jnp.where((gap < -PRUNE_GAP) & (jm == jb), jb + 1, jm)
            jf_ref[gb:gb + 1, :] = jm

    n1, n2, n3 = _split3(-r)
    packed = jnp.where(
        lane < ATT_HEADS, n1.astype(F32),
        jnp.where(lane < 2 * ATT_HEADS, pltpu.roll(n2.astype(F32), ATT_HEADS, 1),
                  jnp.where(lane < 3 * ATT_HEADS, pltpu.roll(n3.astype(F32), 2 * ATT_HEADS, 1), 0.0)))
    nr_ref[...] = packed.astype(BF16)


def _forget(flog, bf_pad, tri, proj):
    s = flog.shape[0]
    nblk = s // FG_T
    return pl.pallas_call(
        functools.partial(_forget_kernel, nblk=nblk),
        grid=(nblk,),
        in_specs=[
            pl.BlockSpec((FG_T, LANES), lambda j: (j, 0)),
            pl.BlockSpec((1, LANES), lambda j: (0, 0)),
            pl.BlockSpec((FG_T, FG_T), lambda j: (0, 0)),
            pl.BlockSpec((FG_T, ATT_W), lambda j: (j, 0)),
            pl.BlockSpec((FG_T, ATT_W), lambda j: (j, 1)),
        ],
        out_specs=[
            pl.BlockSpec((FG_T, LANES), lambda j: (j, 0)),
            pl.BlockSpec((1, SUBLANES, LANES), lambda j: (j, 0, 0)),
            pl.BlockSpec((nblk, LANES), lambda j: (0, 0)),
        ],
        out_shape=[
            jax.ShapeDtypeStruct((s, LANES), BF16),
            jax.ShapeDtypeStruct((nblk, SUBLANES, LANES), F32),
            jax.ShapeDtypeStruct((nblk, LANES), jnp.int32),
        ],
        scratch_shapes=[
            pltpu.VMEM((SUBLANES, LANES), F32),
            pltpu.VMEM((nblk, SUBLANES, LANES), F32),
        ],
        compiler_params=pltpu.CompilerParams(dimension_semantics=("arbitrary",)),
        name="forget",
    )(flog, bf_pad, tri, proj, proj)


def _layernorm(x, g, b):
    mu = jnp.mean(x, axis=-1, keepdims=True)
    xc = x - mu
    var = jnp.mean(xc * xc, axis=-1, keepdims=True)
    return xc * lax.rsqrt(var + EPS) * g + b


def _silu(x):
    return x * jax.nn.sigmoid(x)


def _gelu(x):
    return 0.5 * x * (1.0 + lax.erf(x * (2.0 ** -0.5)))


def _branch_kernel(a_ref, b_ref, ah_ref, bh_ref, zc_ref, u_ref, v_ref, zs_ref,
                   dw_ref, dwb_ref, cg_ref, cb_ref, pw_ref, pwb_ref,
                   sg_ref, sb_ref, ws_ref, bs_ref,
                   o_ref, y_ref, ysh_ref, c_ref):
    i = pl.program_id(0)

    ah = ah_ref[...].astype(F32)
    bh = bh_ref[...].astype(F32)
    hist = ah * jax.nn.sigmoid(bh)
    y_ref[0:CONV_HALO, :] = jnp.where(i == 0, 0.0, hist)
    a = a_ref[...].astype(F32)
    b = b_ref[...].astype(F32)
    y_ref[CONV_HALO:, :] = a * jax.nn.sigmoid(b)

    n_rows = CONV_HALO + BR_TM
    yv = y_ref[...]
    for b in range(1, SUBLANES):
        ysh_ref[b - 1] = pltpu.roll(yv, n_rows - b, 0)

    base = CONV_HALO - (CONV_K - 1)
    assert base >= 0
    for j in range(CONV_K):
        shift, phase = divmod(base + j, SUBLANES)
        assert BR_TM + shift * SUBLANES <= n_rows - phase
    for lb in range(CONV_W // LANES):
        lsl = slice(lb * LANES, (lb + 1) * LANES)
        taps = [jnp.broadcast_to(dw_ref[j:j + 1, lsl], (SUBLANES, LANES)) for j in range(CONV_K)]
        bias = jnp.broadcast_to(dwb_ref[:, lsl], (SUBLANES, LANES))

        for r0 in range(0, BR_TM, SUBLANES):
            acc = bias
            for j in range(CONV_K):
                shift, phase = divmod(base + j, SUBLANES)
                rows = slice(r0 + shift * SUBLANES, r0 + (shift + 1) * SUBLANES)
                src = y_ref[rows, lsl] if phase == 0 else ysh_ref[phase - 1, rows, lsl]
                acc = acc + taps[j] * src
            c_ref[r0:r0 + SUBLANES, lsl] = acc

    yc = _silu(_layernorm(c_ref[...], cg_ref[...], cb_ref[...]))
    yc = jnp.dot(yc.astype(BF16), pw_ref[...], preferred_element_type=F32) + pwb_ref[...]
    zc = zc_ref[...].astype(F32)
    o_ref[:, 0:CONV_W] = (yc * _silu(zc)).astype(BF16)

    vln = _layernorm(_gelu(v_ref[...].astype(F32)), sg_ref[...], sb_ref[...]).astype(BF16)
    tr = lax.broadcasted_iota(jnp.int32, (CHUNK, CHUNK), 0)
    tc = lax.broadcasted_iota(jnp.int32, (CHUNK, CHUNK), 1)
    for hh in range(SGU_HEADS):
        w = jnp.where(tr >= tc, ws_ref[hh], 0.0).astype(BF16)
        for c in range(BR_TM // CHUNK):
            rs = slice(c * CHUNK, (c + 1) * CHUNK)
            cs = slice(hh * HEAD_DIM, (hh + 1) * HEAD_DIM)
            sv = jnp.dot(w, vln[rs, cs], preferred_element_type=F32) + bs_ref[hh]
            c_ref[rs, cs] = sv
    gu = _gelu(u_ref[...].astype(F32))
    zs = zs_ref[...].astype(F32)
    o_ref[:, CONV_W:] = (gu * c_ref[...] * _silu(zs)).astype(BF16)


def _branch(proj, dw, dwb, cg, cb, pw, pwb, sg, sb, ws, bsb):
    s = proj.shape[0]
    w = CONV_W
    c0 = 4 * ATT_W // w
    hpb = BR_TM // CONV_HALO

    def col(k):
        return pl.BlockSpec((BR_TM, w), lambda i, k=k: (i, c0 + k))

    def halo(k):
        return pl.BlockSpec((CONV_HALO, w), lambda i, k=k: (jnp.maximum(i * hpb - 1, 0), c0 + k))

    def full(shape):
        return pl.BlockSpec(shape, lambda i: (0,) * len(shape))

    return pl.pallas_call(
        _branch_kernel,
        grid=(s // BR_TM,),
        in_specs=[
            col(0), col(1), halo(0), halo(1), col(2), col(3), col(4), col(5),
            full((CONV_K, w)), full((1, w)), full((1, w)), full((1, w)), full((w, w)), full((1, w)),
            full((1, w)), full((1, w)), full((SGU_HEADS, CHUNK, CHUNK)), full((SGU_HEADS, CHUNK, HEAD_DIM)),
        ],
        out_specs=pl.BlockSpec((BR_TM, 2 * w), lambda i: (i, 0)),
        out_shape=jax.ShapeDtypeStruct((s, 2 * w), BF16),
        scratch_shapes=[
            pltpu.VMEM((CONV_HALO + BR_TM, w), F32),
            pltpu.VMEM((SUBLANES - 1, CONV_HALO + BR_TM, w), F32),
            pltpu.VMEM((BR_TM, w), F32),
        ],
        compiler_params=pltpu.CompilerParams(
            dimension_semantics=("arbitrary",),
            vmem_limit_bytes=_vmem_limit(40 * 1024 * 1024),
        ),
        name="branch",
    )(proj, proj, proj, proj, proj, proj, proj, proj, dw, dwb, cg, cb, pw, pwb, sg, sb, ws, bsb)


def _attn_kernel(cs_ref, jf_ref, q_ref, k_ref, v_ref, nr_ref, z_ref, o_ref,
                 vt_ref, qa_ref, s_ref, sm_ref, acc_ref, m_ref, l_ref, *, nkb):
    h = pl.program_id(0)
    i = pl.program_id(1)

    @pl.when(i == 0)
    def _():
        def xpose(c, carry):
            blk = v_ref[pl.ds(pl.multiple_of(c * ATT_TK, ATT_TK), ATT_TK), :]
            vt_ref[c] = blk.astype(F32).T.astype(BF16)
            return carry
        lax.fori_loop(0, nkb, xpose, 0)

    qa_ref[0:HEAD_DIM, :] = q_ref[...].astype(F32).T.astype(BF16)
    rr = lax.broadcasted_iota(jnp.int32, (LANES, ATT_TQ), 0)
    pick = (rr == h) | (rr == h + ATT_HEADS) | (rr == h + 2 * ATT_HEADS)
    qa_ref[HEAD_DIM:, :] = jnp.where(pick, 1.0, 0.0).astype(BF16)
    m_ref[...] = jnp.full_like(m_ref, MASK_VALUE)
    l_ref[...] = jnp.zeros_like(l_ref)
    acc_ref[...] = jnp.zeros_like(acc_ref)

    diag = (lax.broadcasted_iota(jnp.int32, (ATT_TK, ATT_TG), 0)
            <= lax.broadcasted_iota(jnp.int32, (ATT_TK, ATT_TG), 1))

    n = i * ATT_GROUPS

    def stat(row, j):
        return cs_ref[(row * ATT_HEADS + h) * nkb + j]

    def scores(j, g):
        koff = pl.multiple_of(j * ATT_TK, ATT_TK)
        ka = jnp.concatenate([k_ref[pl.ds(koff, ATT_TK), :], nr_ref[pl.ds(koff, ATT_TK), :]], axis=1)
        return jnp.dot(ka, qa_ref[:, g * ATT_TG:(g + 1) * ATT_TG], preferred_element_type=F32)

    def stage(slot, j, g):
        s = scores(j, g)
        s_ref[slot, g] = s
        sm_ref[slot, g] = jnp.max(s, axis=0, keepdims=True)

    def staged(slot, j, g):
        return (j, s_ref[slot, g], sm_ref[slot, g], False)

    def update(g, blocks):
        lsl = slice(g * ATT_TG, (g + 1) * ATT_TG)
        c_q = stat(CS_FIRST, n + g)
        m_old = m_ref[:, lsl]
        m_new = m_old
        shifted = []
        for j, s, smax, masked in blocks:
            if masked:
                s = jnp.where(diag, s, MASK_VALUE)
            if masked or smax is None:
                smax = jnp.max(s, axis=0, keepdims=True)
            d = c_q - stat(CS_FIRST, j)
            m_new = jnp.maximum(m_new, smax + d)
            shifted.append((j, s, d))
        alpha = jnp.exp2(m_old - m_new)
        l_new = alpha * l_ref[:, lsl]
        acc = acc_ref[:, lsl] * alpha
        for j, s, d in shifted:
            p = jnp.exp2(s - (m_new - d))
            l_new = l_new + jnp.sum(p, axis=0, keepdims=True)
            acc = acc + jnp.dot(vt_ref[j], p.astype(BF16), preferred_element_type=F32)
        l_ref[:, lsl] = l_new
        acc_ref[:, lsl] = acc
        m_ref[:, lsl] = m_new

    groups = range(ATT_GROUPS)
    j_first = jf_ref[h * nkb + n]
    for g in groups[1:]:
        j_first = jnp.minimum(j_first, jf_ref[h * nkb + n + g])
    jj0 = j_first // 2
    for g in groups:
        stage(0, 2 * jj0, g)

    def body(jj, carry):
        j = 2 * jj
        for g in groups:
            stage(1, j + 1, g)
        for g in groups:
            update(g, [staged(0, j, g)])
        for g in groups:
            stage(0, j + 2, g)
        for g in groups:
            update(g, [staged(1, j + 1, g)])
        return carry

    lax.fori_loop(jj0, n // 2, body, 0)
    for g in groups:
        first = staged(0, n, g) if g > 0 else (n, s_ref[0, g], None, True)
        rest = [(n + dj, scores(n + dj, g), None, dj == g) for dj in range(1, g + 1)]
        update(g, [first] + rest)

    out = (acc_ref[...] / l_ref[...]).T
    z = z_ref[...].astype(F32)
    o_ref[...] = (out * _silu(z)).astype(BF16)


def _attn(cstart, jfirst, proj, nr):
    s = proj.shape[0]
    nkb = s // ATT_TK
    nh = ATT_HEADS
    grid_spec = pltpu.PrefetchScalarGridSpec(
        num_scalar_prefetch=2,
        grid=(nh, s // ATT_TQ),
        in_specs=[
            pl.BlockSpec((ATT_TQ, HEAD_DIM), lambda h, i, cs, jf: (i, h)),
            pl.BlockSpec((s, HEAD_DIM), lambda h, i, cs, jf: (0, nh + h)),
            pl.BlockSpec((s, HEAD_DIM), lambda h, i, cs, jf: (0, 2 * nh + h)),
            pl.BlockSpec((s, LANES), lambda h, i, cs, jf: (0, 0)),
            pl.BlockSpec((ATT_TQ, HEAD_DIM), lambda h, i, cs, jf: (i, 3 * nh + h)),
        ],
        out_specs=pl.BlockSpec((ATT_TQ, HEAD_DIM), lambda h, i, cs, jf: (i, h)),
        scratch_shapes=[
            pltpu.VMEM((nkb, HEAD_DIM, ATT_TK), BF16),
            pltpu.VMEM((2 * HEAD_DIM, ATT_TQ), BF16),
            pltpu.VMEM((2, ATT_GROUPS, ATT_TK, ATT_TG), F32),
            pltpu.VMEM((2, ATT_GROUPS, 1, ATT_TG), F32),
            pltpu.VMEM((HEAD_DIM, ATT_TQ), F32),
            pltpu.VMEM((1, ATT_TQ), F32),
            pltpu.VMEM((1, ATT_TQ), F32),
        ],
    )
    return pl.pallas_call(
        functools.partial(_attn_kernel, nkb=nkb),
        grid_spec=grid_spec,
        out_shape=jax.ShapeDtypeStruct((s, ATT_W), BF16),
        compiler_params=pltpu.CompilerParams(
            dimension_semantics=("arbitrary", "arbitrary"),
            vmem_limit_bytes=_vmem_limit(48 * 1024 * 1024),
        ),
        name="attn",
    )(cstart, jfirst, proj, proj, proj, nr, proj)


def _out_kernel(h_ref, ycs_ref, ya_ref, p_ref, wo_ref, gp_ref, wpg_ref, wpp_ref, o_ref):
    ycat = jnp.concatenate([ycs_ref[...], ya_ref[...]], axis=1)
    y = jnp.dot(ycat, wo_ref[...], preferred_element_type=F32)
    inv = lax.rsqrt(jnp.mean(y * y, axis=-1, keepdims=True) + EPS)
    h1 = h_ref[...] + y * inv * gp_ref[...]
    gate = jax.nn.sigmoid(jnp.dot(h1.astype(BF16), wpg_ref[...], preferred_element_type=F32))
    pp = jnp.dot(p_ref[...].astype(BF16), wpp_ref[...], preferred_element_type=F32)
    o_ref[...] = h1 + gate * pp


def _out(h, ycs, yatt, p, wo, gp, wpg, wpp, layer):
    s = h.shape[0]

    def slab(rows, cols):
        return pl.BlockSpec((None, rows, cols), lambda i: (layer, 0, 0), pipeline_mode=pl.Buffered(1))

    return pl.pallas_call(
        _out_kernel,
        grid=(s // OUT_TM,),
        in_specs=[
            pl.BlockSpec((OUT_TM, D_MODEL), lambda i: (i, 0)),
            pl.BlockSpec((OUT_TM, CONV_W + SGU_W), lambda i: (i, 0)),
            pl.BlockSpec((OUT_TM, ATT_W), lambda i: (i, 0)),
            pl.BlockSpec((None, None, OUT_TM, PLE_DIM), lambda i: (layer, 0, i, 0)),
            slab(D_MODEL, D_MODEL),
            slab(1, D_MODEL),
            slab(D_MODEL, D_MODEL),
            slab(PLE_DIM, D_MODEL),
        ],
        out_specs=pl.BlockSpec((OUT_TM, D_MODEL), lambda i: (i, 0)),
        out_shape=jax.ShapeDtypeStruct((s, D_MODEL), F32),
        compiler_params=pltpu.CompilerParams(
            dimension_semantics=("arbitrary",),
            vmem_limit_bytes=_vmem_limit(52 * 1024 * 1024),
        ),
        name="out",
    )(h, ycs, yatt, p, wo, gp, wpg, wpp)


def kernel(x, p, norm_pre, w_in, b_f, conv_dw, conv_dw_b, conv_ln_g, conv_ln_b, conv_pw, conv_pw_b,
           sgu_ln_g, sgu_ln_b, sgu_w, sgu_b, w_out, norm_post, w_pg, w_pp):
    bsz, seq, _ = x.shape
    depth = w_in.shape[0]
    assert bsz == 1 and seq % PROJ_TM == 0 and seq % ATT_TQ == 0

    n_att = 4 * ATT_W
    colscale = jnp.concatenate(
        [jnp.full((1, ATT_W), HEAD_DIM ** -0.5 * LOG2E, F32), jnp.ones((1, N_MAIN - ATT_W), F32)], axis=1)
    tri = jnp.tril(jnp.ones((FG_T, FG_T), BF16))
    row = lambda v: v.reshape(1, -1)

    w_att = w_in[:, :, :n_att].astype(BF16)
    w_mix = w_in[:, :, n_att + ATT_HEADS:].astype(BF16)
    w_f = jnp.pad(w_in[:, :, n_att:n_att + ATT_HEADS], ((0, 0), (0, 0), (0, LANES - ATT_HEADS))).astype(BF16)
    w_out_b, w_pg_b, w_pp_b, conv_pw_b16 = (w.astype(BF16) for w in (w_out, w_pg, w_pp, conv_pw))
    norm_post_r = norm_post[:, None, :]

    h = x[0]
    for i in range(depth):
        bf_pad = jnp.pad(b_f[i], (0, LANES - ATT_HEADS)).reshape(1, LANES)
        bsb = jnp.broadcast_to(sgu_b[i][:, :, None], (SGU_HEADS, CHUNK, HEAD_DIM))

        proj, flog = _proj(h, row(norm_pre[i]), w_att, w_mix, w_f, colscale, i)
        nr, cs, jf = _forget(flog, bf_pad, tri, proj)
        cstart = jnp.transpose(cs[:, :CS_ROWS, :ATT_HEADS], (1, 2, 0)).reshape(-1)
        jfirst = jf[:, :ATT_HEADS].T.reshape(-1)
        ycs = _branch(proj, conv_dw[i], row(conv_dw_b[i]), row(conv_ln_g[i]), row(conv_ln_b[i]),
                      conv_pw_b16[i], row(conv_pw_b[i]),
                      row(sgu_ln_g[i]), row(sgu_ln_b[i]), sgu_w[i], bsb)
        yatt = _attn(cstart, jfirst, proj, nr)
        h = _out(h, ycs, yatt, p, w_out_b, norm_post_r, w_pg_b, w_pp_b, i)
    return h[None]
```

```python
import functools

import jax
import jax.numpy as jnp
from jax import lax
from jax.experimental import pallas as pl
from jax.experimental.pallas import tpu as pltpu

F32 = jnp.float32
BF16 = jnp.bfloat16

D_MODEL = 2048
HEAD_DIM = 128
ATT_HEADS = 8
ATT_W = ATT_HEADS * HEAD_DIM
CONV_W = 512
SGU_HEADS = 4
SGU_W = SGU_HEADS * HEAD_DIM
CONV_K = 31
CHUNK = 128
PLE_DIM = 256
EPS = 1e-6
N_MAIN = 4 * ATT_W + 3 * CONV_W + 3 * SGU_W

LANES = 128
SUBLANES = 8
VMEM_BYTES_V7X = 64 * 1024 * 1024

PROJ_TM = 1024
PROJ_TN = 1024
FG_T = 512
BR_TM = 256
CONV_HALO = 32
ATT_TK = FG_T
ATT_TG = ATT_TK
ATT_GROUPS = 2
ATT_TQ = ATT_GROUPS * ATT_TG
OUT_TM = 512
MASK_VALUE = -1e30
LOG2E = 1.4426950408889634
CS_FIRST, CS_LAST, CS_QN, CS_KN, CS_ROWS = 0, 1, 2, 3, 4
PRUNE_GAP = 160.0
PRUNE_NORM_SLACK = 1.01


def _vmem_limit(nbytes):
    assert nbytes < VMEM_BYTES_V7X
    return int(nbytes)


def _proj_kernel(x_ref, g_ref, wa_ref, wb_ref, wf_ref, cs_ref, o_ref, f_ref, xn_ref, *, na):
    @pl.when(pl.program_id(1) == 0)
    def _():
        x = x_ref[...]
        inv = lax.rsqrt(jnp.mean(x * x, axis=-1, keepdims=True) + EPS)
        xn_ref[...] = (x * inv * g_ref[...]).astype(BF16)
        f_ref[...] = jnp.dot(xn_ref[...], wf_ref[...], preferred_element_type=F32)

    def emit(w_ref):
        acc = jnp.dot(xn_ref[...], w_ref[...], preferred_element_type=F32)
        o_ref[...] = (acc * cs_ref[...]).astype(BF16)

    pl.when(pl.program_id(1) < na)(lambda: emit(wa_ref))
    pl.when(pl.program_id(1) >= na)(lambda: emit(wb_ref))


def _proj(h, g, w_att, w_mix, w_f, colscale, layer):
    s = h.shape[0]
    na = w_att.shape[2] // PROJ_TN
    nb = w_mix.shape[2] // PROJ_TN
    assert (na + nb) * PROJ_TN == N_MAIN
    grid = (s // PROJ_TM, na + nb)
    return pl.pallas_call(
        functools.partial(_proj_kernel, na=na),
        grid=grid,
        in_specs=[
            pl.BlockSpec((PROJ_TM, D_MODEL), lambda i, j: (i, 0)),
            pl.BlockSpec((1, D_MODEL), lambda i, j: (0, 0)),
            pl.BlockSpec((None, D_MODEL, PROJ_TN), lambda i, j: (layer, 0, jnp.minimum(j, na - 1))),
            pl.BlockSpec((None, D_MODEL, PROJ_TN),
                         lambda i, j: (layer, 0, jnp.where(j < na - 1, nb - 1, jnp.maximum(j - na, 0)))),
            pl.BlockSpec((None, D_MODEL, LANES), lambda i, j: (layer, 0, 0)),
            pl.BlockSpec((1, PROJ_TN), lambda i, j: (0, j)),
        ],
        out_specs=[
            pl.BlockSpec((PROJ_TM, PROJ_TN), lambda i, j: (i, j)),
            pl.BlockSpec((PROJ_TM, LANES), lambda i, j: (i, 0)),
        ],
        out_shape=[
            jax.ShapeDtypeStruct((s, N_MAIN), BF16),
            jax.ShapeDtypeStruct((s, LANES), F32),
        ],
        scratch_shapes=[pltpu.VMEM((PROJ_TM, D_MODEL), BF16)],
        compiler_params=pltpu.CompilerParams(
            dimension_semantics=("arbitrary", "arbitrary"),
            vmem_limit_bytes=_vmem_limit(56 * 1024 * 1024),
        ),
        name="proj",
    )(h, g, w_att, w_mix, w_f, colscale)


def _split3(x):
    p1 = x.astype(BF16)
    r1 = x - p1.astype(F32)
    p2 = r1.astype(BF16)
    p3 = (r1 - p2.astype(F32)).astype(BF16)
    return p1, p2, p3


def _max_row_norm(x_ref):
    lane = lax.broadcasted_iota(jnp.int32, (1, LANES), 1)
    out = jnp.zeros((1, LANES), F32)
    for hh in range(ATT_HEADS):
        x = x_ref[:, hh * HEAD_DIM:(hh + 1) * HEAD_DIM].astype(F32)
        ss = jnp.max(jnp.sum(x * x, axis=-1, keepdims=True), axis=0, keepdims=True)
        out = jnp.where(lane == hh, jnp.sqrt(ss), out)
    return out


def _forget_kernel(f_ref, bf_ref, tri_ref, q_ref, k_ref, nr_ref, cs_ref, jf_ref, carry_ref, st_ref, *, nblk):
    step = pl.program_id(0)

    @pl.when(step == 0)
    def _():
        carry_ref[...] = jnp.zeros_like(carry_ref)

    lane = lax.broadcasted_iota(jnp.int32, (FG_T, LANES), 1)
    row = lax.broadcasted_iota(jnp.int32, (FG_T, LANES), 0)
    z = f_ref[...] + bf_ref[...]
    logf = (jnp.minimum(z, 0.0) - jnp.log1p(jnp.exp(-jnp.abs(z)))) * LOG2E
    logf = jnp.where(lane < ATT_HEADS, logf, 0.0)
    first = logf[0:1, :]
    p1, p2, p3 = _split3(jnp.where(row == 0, 0.0, logf))
    tri = tri_ref[...]
    r = (jnp.dot(tri, p1, preferred_element_type=F32)
         + jnp.dot(tri, p2, preferred_element_type=F32)
         + jnp.dot(tri, p3, preferred_element_type=F32))
    c_first = carry_ref[0:1, :] + first
    c_last = c_first + r[FG_T - 1:FG_T, :]
    carry_ref[...] = jnp.broadcast_to(c_last, carry_ref.shape)
    srow = lax.broadcasted_iota(jnp.int32, (SUBLANES, LANES), 0)
    stats = jnp.where(srow == CS_FIRST, c_first,
                      jnp.where(srow == CS_LAST, c_last,
                                jnp.where(srow == CS_QN, _max_row_norm(q_ref),
                                          jnp.where(srow == CS_KN, _max_row_norm(k_ref), 0.0))))
    cs_ref[...] = stats[None]
    st_ref[step] = stats

    @pl.when(step == nblk - 1)
    def _():
        for gb in range(nblk):
            sg = st_ref[gb]
            qn = sg[CS_QN:CS_QN + 1] * PRUNE_NORM_SLACK
            c_q = sg[CS_FIRST:CS_FIRST + 1]
            kn_self = sg[CS_KN:CS_KN + 1]
            jm = jnp.zeros((1, LANES), jnp.int32)
            for jb in range(gb):
                sj = st_ref[jb]
                gap = qn * (sj[CS_KN:CS_KN + 1] + kn_self) + (c_q - sj[CS_LAST:CS_LAST + 1])
                jm = jnp.where((gap < -PRUNE_GAP) & (jm == jb), jb + 1, jm)
            jf_ref[gb:gb + 1, :] = jm

    n1, n2, n3 = _split3(-r)
    packed = jnp.where(
        lane < ATT_HEADS, n1.astype(F32),
        jnp.where(lane < 2 * ATT_HEADS, pltpu.roll(n2.astype(F32), ATT_HEADS, 1),
                  jnp.where(lane < 3 * ATT_HEADS, pltpu.roll(n3.astype(F32), 2 * ATT_HEADS, 1), 0.0)))
    nr_ref[...] = packed.astype(BF16)


def _forget(flog, bf_pad, tri, proj):
    s = flog.shape[0]
    nblk = s // FG_T
    return pl.pallas_call(
        functools.partial(_forget_kernel, nblk=nblk),
        grid=(nblk,),
        in_specs=[
            pl.BlockSpec((FG_T, LANES), lambda j: (j, 0)),
            pl.BlockSpec((1, LANES), lambda j: (0, 0)),
            pl.BlockSpec((FG_T, FG_T), lambda j: (0, 0)),
            pl.BlockSpec((FG_T, ATT_W), lambda j: (j, 0)),
            pl.BlockSpec((FG_T, ATT_W), lambda j: (j, 1)),
        ],
        out_specs=[
            pl.BlockSpec((FG_T, LANES), lambda j: (j, 0)),
            pl.BlockSpec((1, SUBLANES, LANES), lambda j: (j, 0, 0)),
            pl.BlockSpec((nblk, LANES), lambda j: (0, 0)),
        ],
        out_shape=[
            jax.ShapeDtypeStruct((s, LANES), BF16),
            jax.ShapeDtypeStruct((nblk, SUBLANES, LANES), F32),
            jax.ShapeDtypeStruct((nblk, LANES), jnp.int32),
        ],
        scratch_shapes=[
            pltpu.VMEM((SUBLANES, LANES), F32),
            pltpu.VMEM((nblk, SUBLANES, LANES), F32),
        ],
        compiler_params=pltpu.CompilerParams(dimension_semantics=("arbitrary",)),
        name="forget",
    )(flog, bf_pad, tri, proj, proj)


def _layernorm(x, g, b):
    mu = jnp.mean(x, axis=-1, keepdims=True)
    xc = x - mu
    var = jnp.mean(xc * xc, axis=-1, keepdims=True)
    return xc * lax.rsqrt(var + EPS) * g + b


def _silu(x):
    return x * jax.nn.sigmoid(x)


def _gelu(x):
    return 0.5 * x * (1.0 + lax.erf(x * (2.0 ** -0.5)))


def _branch_kernel(a_ref, b_ref, ah_ref, bh_ref, zc_ref, u_ref, v_ref, zs_ref,
                   dw_ref, dwb_ref, cg_ref, cb_ref, pw_ref, pwb_ref,
                   sg_ref, sb_ref, ws_ref, bs_ref,
                   o_ref, y_ref, ysh_ref, c_ref):
    i = pl.program_id(0)

    ah = ah_ref[...].astype(F32)
    bh = bh_ref[...].astype(F32)
    hist = ah * jax.nn.sigmoid(bh)
    y_ref[0:CONV_HALO, :] = jnp.where(i == 0, 0.0, hist)
    a = a_ref[...].astype(F32)
    b = b_ref[...].astype(F32)
    y_ref[CONV_HALO:, :] = a * jax.nn.sigmoid(b)

    n_rows = CONV_HALO + BR_TM
    yv = y_ref[...]
    for b in range(1, SUBLANES):
        ysh_ref[b - 1] = pltpu.roll(yv, n_rows - b, 0)

    base = CONV_HALO - (CONV_K - 1)
    assert base >= 0
    for j in range(CONV_K):
        shift, phase = divmod(base + j, SUBLANES)
        assert BR_TM + shift * SUBLANES <= n_rows - phase
    for lb in range(CONV_W // LANES):
        lsl = slice(lb * LANES, (lb + 1) * LANES)
        taps = [jnp.broadcast_to(dw_ref[j:j + 1, lsl], (SUBLANES, LANES)) for j in range(CONV_K)]
        bias = jnp.broadcast_to(dwb_ref[:, lsl], (SUBLANES, LANES))

        for r0 in range(0, BR_TM, SUBLANES):
            acc = bias
            for j in range(CONV_K):
                shift, phase = divmod(base + j, SUBLANES)
                rows = slice(r0 + shift * SUBLANES, r0 + (shift + 1) * SUBLANES)
                src = y_ref[rows, lsl] if phase == 0 else ysh_ref[phase - 1, rows, lsl]
                acc = acc + taps[j] * src
            c_ref[r0:r0 + SUBLANES, lsl] = acc

    yc = _silu(_layernorm(c_ref[...], cg_ref[...], cb_ref[...]))
    yc = jnp.dot(yc.astype(BF16), pw_ref[...], preferred_element_type=F32) + pwb_ref[...]
    zc = zc_ref[...].astype(F32)
    o_ref[:, 0:CONV_W] = (yc * _silu(zc)).astype(BF16)

    vln = _layernorm(_gelu(v_ref[...].astype(F32)), sg_ref[...], sb_ref[...]).astype(BF16)
    tr = lax.broadcasted_iota(jnp.int32, (CHUNK, CHUNK), 0)
    tc = lax.broadcasted_iota(jnp.int32, (CHUNK, CHUNK), 1)
    for hh in range(SGU_HEADS):
        w = jnp.where(tr >= tc, ws_ref[hh], 0.0).astype(BF16)
        for c in range(BR_TM // CHUNK):
            rs = slice(c * CHUNK, (c + 1) * CHUNK)
            cs = slice(hh * HEAD_DIM, (hh + 1) * HEAD_DIM)
            sv = jnp.dot(w, vln[rs, cs], preferred_element_type=F32) + bs_ref[hh]
            c_ref[rs, cs] = sv
    gu = _gelu(u_ref[...].astype(F32))
    zs = zs_ref[...].astype(F32)
    o_ref[:, CONV_W:] = (gu * c_ref[...] * _silu(zs)).astype(BF16)


def _branch(proj, dw, dwb, cg, cb, pw, pwb, sg, sb, ws, bsb):
    s = proj.shape[0]
    w = CONV_W
    c0 = 4 * ATT_W // w
    hpb = BR_TM // CONV_HALO

    def col(k):
        return pl.BlockSpec((BR_TM, w), lambda i, k=k: (i, c0 + k))

    def halo(k):
        return pl.BlockSpec((CONV_HALO, w), lambda i, k=k: (jnp.maximum(i * hpb - 1, 0), c0 + k))

    def full(shape):
        return pl.BlockSpec(shape, lambda i: (0,) * len(shape))

    return pl.pallas_call(
        _branch_kernel,
        grid=(s // BR_TM,),
        in_specs=[
            col(0), col(1), halo(0), halo(1), col(2), col(3), col(4), col(5),
            full((CONV_K, w)), full((1, w)), full((1, w)), full((1, w)), full((w, w)), full((1, w)),
            full((1, w)), full((1, w)), full((SGU_HEADS, CHUNK, CHUNK)), full((SGU_HEADS, CHUNK, HEAD_DIM)),
        ],
        out_specs=pl.BlockSpec((BR_TM, 2 * w), lambda i: (i, 0)),
        out_shape=jax.ShapeDtypeStruct((s, 2 * w), BF16),
        scratch_shapes=[
            pltpu.VMEM((CONV_HALO + BR_TM, w), F32),
            pltpu.VMEM((SUBLANES - 1, CONV_HALO + BR_TM, w), F32),
            pltpu.VMEM((BR_TM, w), F32),
        ],
        compiler_params=pltpu.CompilerParams(
            dimension_semantics=("arbitrary",),
            vmem_limit_bytes=_vmem_limit(40 * 1024 * 1024),
        ),
        name="branch",
    )(proj, proj, proj, proj, proj, proj, proj, proj, dw, dwb, cg, cb, pw, pwb, sg, sb, ws, bsb)


def _attn_kernel(cs_ref, jf_ref, q_ref, k_ref, v_ref, nr_ref, z_ref, o_ref,
                 vt_ref, qa_ref, s_ref, sm_ref, acc_ref, m_ref, l_ref, *, nkb):
    h = pl.program_id(0)
    i = pl.program_id(1)

    @pl.when(i == 0)
    def _():
        def xpose(c, carry):
            blk = v_ref[pl.ds(pl.multiple_of(c * ATT_TK, ATT_TK), ATT_TK), :]
            vt_ref[c] = blk.astype(F32).T.astype(BF16)
            return carry
        lax.fori_loop(0, nkb, xpose, 0)

    qa_ref[0:HEAD_DIM, :] = q_ref[...].astype(F32).T.astype(BF16)
    rr = lax.broadcasted_iota(jnp.int32, (LANES, ATT_TQ), 0)
    pick = (rr == h) | (rr == h + ATT_HEADS) | (rr == h + 2 * ATT_HEADS)
    qa_ref[HEAD_DIM:, :] = jnp.where(pick, 1.0, 0.0).astype(BF16)
    m_ref[...] = jnp.full_like(m_ref, MASK_VALUE)
    l_ref[...] = jnp.zeros_like(l_ref)
    acc_ref[...] = jnp.zeros_like(acc_ref)

    diag = (lax.broadcasted_iota(jnp.int32, (ATT_TK, ATT_TG), 0)
            <= lax.broadcasted_iota(jnp.int32, (ATT_TK, ATT_TG), 1))

    n = i * ATT_GROUPS

    def stat(row, j):
        return cs_ref[(row * ATT_HEADS + h) * nkb + j]

    def scores(j, g):
        koff = pl.multiple_of(j * ATT_TK, ATT_TK)
        ka = jnp.concatenate([k_ref[pl.ds(koff, ATT_TK), :], nr_ref[pl.ds(koff, ATT_TK), :]], axis=1)
        return jnp.dot(ka, qa_ref[:, g * ATT_TG:(g + 1) * ATT_TG], preferred_element_type=F32)

    def stage(slot, j, g):
        s = scores(j, g)
        s_ref[slot, g] = s
        sm_ref[slot, g] = jnp.max(s, axis=0, keepdims=True)

    def staged(slot, j, g):
        return (j, s_ref[slot, g], sm_ref[slot, g], False)

    def update(g, blocks):
        lsl = slice(g * ATT_TG, (g + 1) * ATT_TG)
        c_q = stat(CS_FIRST, n + g)
        m_old = m_ref[:, lsl]
        m_new = m_old
        shifted = []
        for j, s, smax, masked in blocks:
            if masked:
                s = jnp.where(diag, s, MASK_VALUE)
            if masked or smax is None:
                smax = jnp.max(s, axis=0, keepdims=True)
            d = c_q - stat(CS_FIRST, j)
            m_new = jnp.maximum(m_new, smax + d)
            shifted.append((j, s, d))
        alpha = jnp.exp2(m_old - m_new)
        l_new = alpha * l_ref[:, lsl]
        acc = acc_ref[:, lsl] * alpha
        for j, s, d in shifted:
            p = jnp.exp2(s - (m_new - d))
            l_new = l_new + jnp.sum(p, axis=0, keepdims=True)
            acc = acc + jnp.dot(vt_ref[j], p.astype(BF16), preferred_element_type=F32)
        l_ref[:, lsl] = l_new
        acc_ref[:, lsl] = acc
        m_ref[:, lsl] = m_new

    groups = range(ATT_GROUPS)
    j_first = jf_ref[h * nkb + n]
    for g in groups[1:]:
        j_first = jnp.minimum(j_first, jf_ref[h * nkb + n + g])
    jj0 = j_first // 2
    for g in groups:
        stage(0, 2 * jj0, g)

    def body(jj, carry):
        j = 2 * jj
        for g in groups:
            stage(1, j + 1, g)
            update(g, [staged(0, j, g)])
        for g in groups:
            stage(0, j + 2, g)
            update(g, [staged(1, j + 1, g)])
        return carry

    lax.fori_loop(jj0, n // 2, body, 0)
    rest = {g: [(n + dj, scores(n + dj, g), None, dj == g) for dj in range(1, g + 1)] for g in groups}
    for g in groups:
        first = staged(0, n, g) if g > 0 else (n, s_ref[0, g], None, True)
        update(g, [first] + rest[g])

    out = (acc_ref[...] / l_ref[...]).T
    z = z_ref[...].astype(F32)
    o_ref[...] = (out * _silu(z)).astype(BF16)


def _attn(cstart, jfirst, proj, nr):
    s = proj.shape[0]
    nkb = s // ATT_TK
    nh = ATT_HEADS
    grid_spec = pltpu.PrefetchScalarGridSpec(
        num_scalar_prefetch=2,
        grid=(nh, s // ATT_TQ),
        in_specs=[
            pl.BlockSpec((ATT_TQ, HEAD_DIM), lambda h, i, cs, jf: (i, h)),
            pl.BlockSpec((s, HEAD_DIM), lambda h, i, cs, jf: (0, nh + h)),
            pl.BlockSpec((s, HEAD_DIM), lambda h, i, cs, jf: (0, 2 * nh + h)),
            pl.BlockSpec((s, LANES), lambda h, i, cs, jf: (0, 0)),
            pl.BlockSpec((ATT_TQ, HEAD_DIM), lambda h, i, cs, jf: (i, 3 * nh + h)),
        ],
        out_specs=pl.BlockSpec((ATT_TQ, HEAD_DIM), lambda h, i, cs, jf: (i, h)),
        scratch_shapes=[
            pltpu.VMEM((nkb, HEAD_DIM, ATT_TK), BF16),
            pltpu.VMEM((2 * HEAD_DIM, ATT_TQ), BF16),
            pltpu.VMEM((2, ATT_GROUPS, ATT_TK, ATT_TG), F32),
            pltpu.VMEM((2, ATT_GROUPS, 1, ATT_TG), F32),
            pltpu.VMEM((HEAD_DIM, ATT_TQ), F32),
            pltpu.VMEM((1, ATT_TQ), F32),
            pltpu.VMEM((1, ATT_TQ), F32),
        ],
    )
    return pl.pallas_call(
        functools.partial(_attn_kernel, nkb=nkb),
        grid_spec=grid_spec,
        out_shape=jax.ShapeDtypeStruct((s, ATT_W), BF16),
        compiler_params=pltpu.CompilerParams(
            dimension_semantics=("arbitrary", "arbitrary"),
            vmem_limit_bytes=_vmem_limit(48 * 1024 * 1024),
        ),
        name="attn",
    )(cstart, jfirst, proj, proj, proj, nr, proj)


def _out_kernel(h_ref, ycs_ref, ya_ref, p_ref, wo_ref, gp_ref, wpg_ref, wpp_ref, o_ref):
    ycat = jnp.concatenate([ycs_ref[...], ya_ref[...]], axis=1)
    y = jnp.dot(ycat, wo_ref[...], preferred_element_type=F32)
    inv = lax.rsqrt(jnp.mean(y * y, axis=-1, keepdims=True) + EPS)
    h1 = h_ref[...] + y * inv * gp_ref[...]
    gate = jax.nn.sigmoid(jnp.dot(h1.astype(BF16), wpg_ref[...], preferred_element_type=F32))
    pp = jnp.dot(p_ref[...].astype(BF16), wpp_ref[...], preferred_element_type=F32)
    o_ref[...] = h1 + gate * pp


def _out(h, ycs, yatt, p, wo, gp, wpg, wpp, layer):
    s = h.shape[0]

    def slab(rows, cols):
        return pl.BlockSpec((None, rows, cols), lambda i: (layer, 0, 0), pipeline_mode=pl.Buffered(1))

    return pl.pallas_call(
        _out_kernel,
        grid=(s // OUT_TM,),
        in_specs=[
            pl.BlockSpec((OUT_TM, D_MODEL), lambda i: (i, 0)),
            pl.BlockSpec((OUT_TM, CONV_W + SGU_W), lambda i: (i, 0)),
            pl.BlockSpec((OUT_TM, ATT_W), lambda i: (i, 0)),
            pl.BlockSpec((None, None, OUT_TM, PLE_DIM), lambda i: (layer, 0, i, 0)),
            slab(D_MODEL, D_MODEL),
            slab(1, D_MODEL),
            slab(D_MODEL, D_MODEL),
            slab(PLE_DIM, D_MODEL),
        ],
        out_specs=pl.BlockSpec((OUT_TM, D_MODEL), lambda i: (i, 0)),
        out_shape=jax.ShapeDtypeStruct((s, D_MODEL), F32),
        compiler_params=pltpu.CompilerParams(
            dimension_semantics=("arbitrary",),
            vmem_limit_bytes=_vmem_limit(52 * 1024 * 1024),
        ),
        name="out",
    )(h, ycs, yatt, p, wo, gp, wpg, wpp)


def kernel(x, p, norm_pre, w_in, b_f, conv_dw, conv_dw_b, conv_ln_g, conv_ln_b, conv_pw, conv_pw_b,
           sgu_ln_g, sgu_ln_b, sgu_w, sgu_b, w_out, norm_post, w_pg, w_pp):
    bsz, seq, _ = x.shape
    depth = w_in.shape[0]
    assert bsz == 1 and seq % PROJ_TM == 0 and seq % ATT_TQ == 0

    n_att = 4 * ATT_W
    colscale = jnp.concatenate(
        [jnp.full((1, ATT_W), HEAD_DIM ** -0.5 * LOG2E, F32), jnp.ones((1, N_MAIN - ATT_W), F32)], axis=1)
    tri = jnp.tril(jnp.ones((FG_T, FG_T), BF16))
    row = lambda v: v.reshape(1, -1)

    w_att = w_in[:, :, :n_att].astype(BF16)
    w_mix = w_in[:, :, n_att + ATT_HEADS:].astype(BF16)
    w_f = jnp.pad(w_in[:, :, n_att:n_att + ATT_HEADS], ((0, 0), (0, 0), (0, LANES - ATT_HEADS))).astype(BF16)
    w_out_b, w_pg_b, w_pp_b, conv_pw_b16 = (w.astype(BF16) for w in (w_out, w_pg, w_pp, conv_pw))
    norm_post_r = norm_post[:, None, :]

    h = x[0]
    for i in range(depth):
        bf_pad = jnp.pad(b_f[i], (0, LANES - ATT_HEADS)).reshape(1, LANES)
        bsb = jnp.broadcast_to(sgu_b[i][:, :, None], (SGU_HEADS, CHUNK, HEAD_DIM))

        proj, flog = _proj(h, row(norm_pre[i]), w_att, w_mix, w_f, colscale, i)
        nr, cs, jf = _forget(flog, bf_pad, tri, proj)
        cstart = jnp.transpose(cs[:, :CS_ROWS, :ATT_HEADS], (1, 2, 0)).reshape(-1)
        jfirst = jf[:, :ATT_HEADS].T.reshape(-1)
        ycs = _branch(proj, conv_dw[i], row(conv_dw_b[i]), row(conv_ln_g[i]), row(conv_ln_b[i]),
                      conv_pw_b16[i], row(conv_pw_b[i]),
                      row(sgu_ln_g[i]), row(sgu_ln_b[i]), sgu_w[i], bsb)
        yatt = _attn(cstart, jfirst, proj, nr)
        h = _out(h, ycs, yatt, p, w_out_b, norm_post_r, w_pg_b, w_pp_b, i)
    return h[None]
```

```python
import functools

import jax
import jax.numpy as jnp
from jax import lax
from jax.experimental import pallas as pl
from jax.experimental.pallas import tpu as pltpu

F32 = jnp.float32
BF16 = jnp.bfloat16

D_MODEL = 2048
HEAD_DIM = 128
ATT_HEADS = 8
ATT_W = ATT_HEADS * HEAD_DIM
CONV_W = 512
SGU_HEADS = 4
SGU_W = SGU_HEADS * HEAD_DIM
CONV_K = 31
CHUNK = 128
PLE_DIM = 256
EPS = 1e-6
N_MAIN = 4 * ATT_W + 3 * CONV_W + 3 * SGU_W

LANES = 128
SUBLANES = 8
VMEM_BYTES_V7X = 64 * 1024 * 1024

PROJ_TM = 1024
PROJ_TN = 1024
FG_T = 512
BR_TM = 256
CONV_HALO = 32
ATT_TK = FG_T
ATT_TG = ATT_TK
ATT_GROUPS = 4
ATT_TQ = ATT_GROUPS * ATT_TG
OUT_TM = 512
MASK_VALUE = -1e30
LOG2E = 1.4426950408889634
CS_FIRST, CS_LAST, CS_QN, CS_KN, CS_ROWS = 0, 1, 2, 3, 4
PRUNE_GAP = 160.0
PRUNE_NORM_SLACK = 1.01


def _vmem_limit(nbytes):
    assert nbytes < VMEM_BYTES_V7X
    return int(nbytes)


def _snake(i, j, n):
    return jnp.where(i % 2 == 0, j, n - 1 - j)


def _proj_kernel(x_ref, g_ref, wa_ref, wb_ref, wf_ref, cs_ref, o_ref, f_ref, xn_ref, *, na, nb):
    @pl.when(pl.program_id(1) == 0)
    def _():
        x = x_ref[...]
        inv = lax.rsqrt(jnp.mean(x * x, axis=-1, keepdims=True) + EPS)
        xn_ref[...] = (x * inv * g_ref[...]).astype(BF16)
        f_ref[...] = jnp.dot(xn_ref[...], wf_ref[...], preferred_element_type=F32)

    def emit(w_ref):
        acc = jnp.dot(xn_ref[...], w_ref[...], preferred_element_type=F32)
        o_ref[...] = (acc * cs_ref[...]).astype(BF16)

    col = _snake(pl.program_id(0), pl.program_id(1), na + nb)
    pl.when(col < na)(lambda: emit(wa_ref))
    pl.when(col >= na)(lambda: emit(wb_ref))


def _proj(h, g, w_att, w_mix, w_f, colscale, layer):
    s = h.shape[0]
    na = 4 * ATT_W // PROJ_TN
    nb = w_mix.shape[2] // PROJ_TN
    nt = na + nb
    assert nt * PROJ_TN == N_MAIN
    grid = (s // PROJ_TM, nt)
    return pl.pallas_call(
        functools.partial(_proj_kernel, na=na, nb=nb),
        grid=grid,
        in_specs=[
            pl.BlockSpec((PROJ_TM, D_MODEL), lambda i, j: (i, 0)),
            pl.BlockSpec((1, D_MODEL), lambda i, j: (0, 0)),
            pl.BlockSpec((None, D_MODEL, PROJ_TN), lambda i, j: (layer, 0, jnp.minimum(_snake(i, j, nt), na - 1))),
            pl.BlockSpec((None, D_MODEL, PROJ_TN), lambda i, j: (layer, 0, jnp.maximum(_snake(i, j, nt) - na, 0))),
            pl.BlockSpec((None, D_MODEL, LANES), lambda i, j: (layer, 0, 0)),
            pl.BlockSpec((1, PROJ_TN), lambda i, j: (0, _snake(i, j, nt))),
        ],
        out_specs=[
            pl.BlockSpec((PROJ_TM, PROJ_TN), lambda i, j: (i, _snake(i, j, nt))),
            pl.BlockSpec((PROJ_TM, LANES), lambda i, j: (i, 0)),
        ],
        out_shape=[
            jax.ShapeDtypeStruct((s, N_MAIN), BF16),
            jax.ShapeDtypeStruct((s, LANES), F32),
        ],
        scratch_shapes=[pltpu.VMEM((PROJ_TM, D_MODEL), BF16)],
        compiler_params=pltpu.CompilerParams(
            dimension_semantics=("arbitrary", "arbitrary"),
            vmem_limit_bytes=_vmem_limit(56 * 1024 * 1024),
        ),
        name="proj",
    )(h, g, w_att, w_mix, w_f, colscale)


def _split3(x):
    p1 = x.astype(BF16)
    r1 = x - p1.astype(F32)
    p2 = r1.astype(BF16)
    p3 = (r1 - p2.astype(F32)).astype(BF16)
    return p1, p2, p3


def _max_row_norm(x_ref):
    lane = lax.broadcasted_iota(jnp.int32, (1, LANES), 1)
    out = jnp.zeros((1, LANES), F32)
    for hh in range(ATT_HEADS):
        x = x_ref[:, hh * HEAD_DIM:(hh + 1) * HEAD_DIM].astype(F32)
        ss = jnp.max(jnp.sum(x * x, axis=-1, keepdims=True), axis=0, keepdims=True)
        out = jnp.where(lane == hh, jnp.sqrt(ss), out)
    return out


def _forget_kernel(f_ref, bf_ref, tri_ref, q_ref, k_ref, nr_ref, cs_ref, jf_ref, carry_ref, st_ref, *, nblk):
    step = pl.program_id(0)

    @pl.when(step == 0)
    def _():
        carry_ref[...] = jnp.zeros_like(carry_ref)

    lane = lax.broadcasted_iota(jnp.int32, (FG_T, LANES), 1)
    row = lax.broadcasted_iota(jnp.int32, (FG_T, LANES), 0)
    z = f_ref[...] + bf_ref[...]
    logf = (jnp.minimum(z, 0.0) - jnp.log1p(jnp.exp(-jnp.abs(z)))) * LOG2E
    logf = jnp.where(lane < ATT_HEADS, logf, 0.0)
    first = logf[0:1, :]
    p1, p2, p3 = _split3(jnp.where(row == 0, 0.0, logf))
    tri = tri_ref[...]
    r = (jnp.dot(tri, p1, preferred_element_type=F32)
         + jnp.dot(tri, p2, preferred_element_type=F32)
         + jnp.dot(tri, p3, preferred_element_type=F32))
    c_first = carry_ref[0:1, :] + first
    c_last = c_first + r[FG_T - 1:FG_T, :]
    carry_ref[...] = jnp.broadcast_to(c_last, carry_ref.shape)
    srow = lax.broadcasted_iota(jnp.int32, (SUBLANES, LANES), 0)
    stats = jnp.where(srow == CS_FIRST, c_first,
                      jnp.where(srow == CS_LAST, c_last,
                                jnp.where(srow == CS_QN, _max_row_norm(q_ref),
                                          jnp.where(srow == CS_KN, _max_row_norm(k_ref), 0.0))))
    cs_ref[...] = stats[None]
    st_ref[step] = stats

    @pl.when(step == nblk - 1)
    def _():
        for gb in range(nblk):
            sg = st_ref[gb]
            qn = sg[CS_QN:CS_QN + 1] * PRUNE_NORM_SLACK
            c_q = sg[CS_FIRST:CS_FIRST + 1]
            kn_self = sg[CS_KN:CS_KN + 1]
            jm = jnp.zeros((1, LANES), jnp.int32)
            for jb in range(gb):
                sj = st_ref[jb]
                gap = qn * (sj[CS_KN:CS_KN + 1] + kn_self) + (c_q - sj[CS_LAST:CS_LAST + 1])
                jm = jnp.where((gap < -PRUNE_GAP) & (jm == jb), jb + 1, jm)
            jf_ref[gb:gb + 1, :] = jm

    n1, n2, n3 = _split3(-r)
    packed = jnp.where(
        lane < ATT_HEADS, n1.astype(F32),
        jnp.where(lane < 2 * ATT_HEADS, pltpu.roll(n2.astype(F32), ATT_HEADS, 1),
                  jnp.where(lane < 3 * ATT_HEADS, pltpu.roll(n3.astype(F32), 2 * ATT_HEADS, 1), 0.0)))
    nr_ref[...] = packed.astype(BF16)


def _forget(flog, bf_pad, tri, proj):
    s = flog.shape[0]
    nblk = s // FG_T
    return pl.pallas_call(
        functools.partial(_forget_kernel, nblk=nblk),
        grid=(nblk,),
        in_specs=[
            pl.BlockSpec((FG_T, LANES), lambda j: (j, 0)),
            pl.BlockSpec((1, LANES), lambda j: (0, 0)),
            pl.BlockSpec((FG_T, FG_T), lambda j: (0, 0)),
            pl.BlockSpec((FG_T, ATT_W), lambda j: (j, 0)),
            pl.BlockSpec((FG_T, ATT_W), lambda j: (j, 1)),
        ],
        out_specs=[
            pl.BlockSpec((FG_T, LANES), lambda j: (j, 0)),
            pl.BlockSpec((1, SUBLANES, LANES), lambda j: (j, 0, 0)),
            pl.BlockSpec((nblk, LANES), lambda j: (0, 0)),
        ],
        out_shape=[
            jax.ShapeDtypeStruct((s, LANES), BF16),
            jax.ShapeDtypeStruct((nblk, SUBLANES, LANES), F32),
            jax.ShapeDtypeStruct((nblk, LANES), jnp.int32),
        ],
        scratch_shapes=[
            pltpu.VMEM((SUBLANES, LANES), F32),
            pltpu.VMEM((nblk, SUBLANES, LANES), F32),
        ],
        compiler_params=pltpu.CompilerParams(dimension_semantics=("arbitrary",)),
        name="forget",
    )(flog, bf_pad, tri, proj, proj)


def _layernorm(x, g, b):
    mu = jnp.mean(x, axis=-1, keepdims=True)
    xc = x - mu
    var = jnp.mean(xc * xc, axis=-1, keepdims=True)
    return xc * lax.rsqrt(var + EPS) * g + b


def _silu(x):
    return x * jax.nn.sigmoid(x)


def _gelu(x):
    return 0.5 * x * (1.0 + lax.erf(x * (2.0 ** -0.5)))


def _branch_kernel(a_ref, b_ref, ah_ref, bh_ref, zc_ref, u_ref, v_ref, zs_ref,
                   dw_ref, dwb_ref, cg_ref, cb_ref, pw_ref, pwb_ref,
                   sg_ref, sb_ref, ws_ref, bs_ref,
                   o_ref, y_ref, ysh_ref, c_ref):
    i = pl.program_id(0)

    ah = ah_ref[...].astype(F32)
    bh = bh_ref[...].astype(F32)
    hist = ah * jax.nn.sigmoid(bh)
    y_ref[0:CONV_HALO, :] = jnp.where(i == 0, 0.0, hist)
    a = a_ref[...].astype(F32)
    b = b_ref[...].astype(F32)
    y_ref[CONV_HALO:, :] = a * jax.nn.sigmoid(b)

    n_rows = CONV_HALO + BR_TM
    yv = y_ref[...]
    for b in range(1, SUBLANES):
        ysh_ref[b - 1] = pltpu.roll(yv, n_rows - b, 0)

    base = CONV_HALO - (CONV_K - 1)
    assert base >= 0
    for j in range(CONV_K):
        shift, phase = divmod(base + j, SUBLANES)
        assert BR_TM + shift * SUBLANES <= n_rows - phase
    for lb in range(CONV_W // LANES):
        lsl = slice(lb * LANES, (lb + 1) * LANES)
        taps = [jnp.broadcast_to(dw_ref[j:j + 1, lsl], (SUBLANES, LANES)) for j in range(CONV_K)]
        bias = jnp.broadcast_to(dwb_ref[:, lsl], (SUBLANES, LANES))

        for r0 in range(0, BR_TM, SUBLANES):
            acc = bias
            for j in range(CONV_K):
                shift, phase = divmod(base + j, SUBLANES)
                rows = slice(r0 + shift * SUBLANES, r0 + (shift + 1) * SUBLANES)
                src = y_ref[rows, lsl] if phase == 0 else ysh_ref[phase - 1, rows, lsl]
                acc = acc + taps[j] * src
            c_ref[r0:r0 + SUBLANES, lsl] = acc

    yc = _silu(_layernorm(c_ref[...], cg_ref[...], cb_ref[...]))
    yc = jnp.dot(yc.astype(BF16), pw_ref[...], preferred_element_type=F32) + pwb_ref[...]
    zc = zc_ref[...].astype(F32)
    o_ref[:, 0:CONV_W] = (yc * _silu(zc)).astype(BF16)

    vln = _layernorm(_gelu(v_ref[...].astype(F32)), sg_ref[...], sb_ref[...]).astype(BF16)
    tr = lax.broadcasted_iota(jnp.int32, (CHUNK, CHUNK), 0)
    tc = lax.broadcasted_iota(jnp.int32, (CHUNK, CHUNK), 1)
    for hh in range(SGU_HEADS):
        w = jnp.where(tr >= tc, ws_ref[hh], 0.0).astype(BF16)
        for c in range(BR_TM // CHUNK):
            rs = slice(c * CHUNK, (c + 1) * CHUNK)
            cs = slice(hh * HEAD_DIM, (hh + 1) * HEAD_DIM)
            sv = jnp.dot(w, vln[rs, cs], preferred_element_type=F32) + bs_ref[hh]
            c_ref[rs, cs] = sv
    gu = _gelu(u_ref[...].astype(F32))
    zs = zs_ref[...].astype(F32)
    o_ref[:, CONV_W:] = (gu * c_ref[...] * _silu(zs)).astype(BF16)


def _branch(proj, dw, dwb, cg, cb, pw, pwb, sg, sb, ws, bsb):
    s = proj.shape[0]
    w = CONV_W
    c0 = 4 * ATT_W // w
    hpb = BR_TM // CONV_HALO

    def col(k):
        return pl.BlockSpec((BR_TM, w), lambda i, k=k: (i, c0 + k))

    def halo(k):
        return pl.BlockSpec((CONV_HALO, w), lambda i, k=k: (jnp.maximum(i * hpb - 1, 0), c0 + k))

    def full(shape):
        return pl.BlockSpec(shape, lambda i: (0,) * len(shape))

    return pl.pallas_call(
        _branch_kernel,
        grid=(s // BR_TM,),
        in_specs=[
            col(0), col(1), halo(0), halo(1), col(2), col(3), col(4), col(5),
            full((CONV_K, w)), full((1, w)), full((1, w)), full((1, w)), full((w, w)), full((1, w)),
            full((1, w)), full((1, w)), full((SGU_HEADS, CHUNK, CHUNK)), full((SGU_HEADS, CHUNK, HEAD_DIM)),
        ],
        out_specs=pl.BlockSpec((BR_TM, 2 * w), lambda i: (i, 0)),
        out_shape=jax.ShapeDtypeStruct((s, 2 * w), BF16),
        scratch_shapes=[
            pltpu.VMEM((CONV_HALO + BR_TM, w), F32),
            pltpu.VMEM((SUBLANES - 1, CONV_HALO + BR_TM, w), F32),
            pltpu.VMEM((BR_TM, w), F32),
        ],
        compiler_params=pltpu.CompilerParams(
            dimension_semantics=("arbitrary",),
            vmem_limit_bytes=_vmem_limit(40 * 1024 * 1024),
        ),
        name="branch",
    )(proj, proj, proj, proj, proj, proj, proj, proj, dw, dwb, cg, cb, pw, pwb, sg, sb, ws, bsb)


def _attn_kernel(cs_ref, jf_ref, q_ref, k_ref, v_ref, nr_ref, z_ref, o_ref,
                 vt_ref, qa_ref, s_ref, sm_ref, acc_ref, m_ref, l_ref, *, nkb):
    h = pl.program_id(0)
    i = pl.program_id(1)

    @pl.when(i == 0)
    def _():
        def xpose(c, carry):
            blk = v_ref[pl.ds(pl.multiple_of(c * ATT_TK, ATT_TK), ATT_TK), :]
            vt_ref[c] = blk.astype(F32).T.astype(BF16)
            return carry
        lax.fori_loop(0, nkb, xpose, 0)

    qa_ref[0:HEAD_DIM, :] = q_ref[...].astype(F32).T.astype(BF16)
    rr = lax.broadcasted_iota(jnp.int32, (LANES, ATT_TQ), 0)
    pick = (rr == h) | (rr == h + ATT_HEADS) | (rr == h + 2 * ATT_HEADS)
    qa_ref[HEAD_DIM:, :] = jnp.where(pick, 1.0, 0.0).astype(BF16)
    m_ref[...] = jnp.full_like(m_ref, MASK_VALUE)
    l_ref[...] = jnp.zeros_like(l_ref)
    acc_ref[...] = jnp.zeros_like(acc_ref)

    diag = (lax.broadcasted_iota(jnp.int32, (ATT_TK, ATT_TG), 0)
            <= lax.broadcasted_iota(jnp.int32, (ATT_TK, ATT_TG), 1))

    n = i * ATT_GROUPS

    def stat(row, j):
        return cs_ref[(row * ATT_HEADS + h) * nkb + j]

    def scores(j, g):
        koff = pl.multiple_of(j * ATT_TK, ATT_TK)
        ka = jnp.concatenate([k_ref[pl.ds(koff, ATT_TK), :], nr_ref[pl.ds(koff, ATT_TK), :]], axis=1)
        return jnp.dot(ka, qa_ref[:, g * ATT_TG:(g + 1) * ATT_TG], preferred_element_type=F32)

    def stage(slot, j, g):
        s = scores(j, g)
        s_ref[slot, g] = s
        sm_ref[slot, g] = jnp.max(s, axis=0, keepdims=True)

    def staged(slot, j, g):
        return (j, s_ref[slot, g], sm_ref[slot, g], False)

    def update(g, blocks):
        lsl = slice(g * ATT_TG, (g + 1) * ATT_TG)
        c_q = stat(CS_FIRST, n + g)
        m_old = m_ref[:, lsl]
        m_new = m_old
        shifted = []
        for j, s, smax, masked in blocks:
            if masked:
                s = jnp.where(diag, s, MASK_VALUE)
            if masked or smax is None:
                smax = jnp.max(s, axis=0, keepdims=True)
            d = c_q - stat(CS_FIRST, j)
            m_new = jnp.maximum(m_new, smax + d)
            shifted.append((j, s, d))
        alpha = jnp.exp2(m_old - m_new)
        l_new = alpha * l_ref[:, lsl]
        acc = acc_ref[:, lsl] * alpha
        for j, s, d in shifted:
            p = jnp.exp2(s - (m_new - d))
            l_new = l_new + jnp.sum(p, axis=0, keepdims=True)
            acc = acc + jnp.dot(vt_ref[j], p.astype(BF16), preferred_element_type=F32)
        l_ref[:, lsl] = l_new
        acc_ref[:, lsl] = acc
        m_ref[:, lsl] = m_new

    groups = range(ATT_GROUPS)
    j_first = jf_ref[h * nkb + n]
    for g in groups[1:]:
        j_first = jnp.minimum(j_first, jf_ref[h * nkb + n + g])
    jj0 = j_first // 2
    for g in groups:
        stage(0, 2 * jj0, g)

    def body(jj, carry):
        j = 2 * jj
        for g in groups:
            stage(1, j + 1, g)
            update(g, [staged(0, j, g)])
        for g in groups:
            stage(0, j + 2, g)
            update(g, [staged(1, j + 1, g)])
        return carry

    lax.fori_loop(jj0, n // 2, body, 0)
    rest = {g: [(n + dj, scores(n + dj, g), None, dj == g) for dj in range(1, g + 1)] for g in groups}
    for g in groups:
        first = staged(0, n, g) if g > 0 else (n, s_ref[0, g], None, True)
        update(g, [first] + rest[g])

    out = (acc_ref[...] / l_ref[...]).T
    z = z_ref[...].astype(F32)
    o_ref[...] = (out * _silu(z)).astype(BF16)


def _attn(cstart, jfirst, proj, nr):
    s = proj.shape[0]
    nkb = s // ATT_TK
    nh = ATT_HEADS
    grid_spec = pltpu.PrefetchScalarGridSpec(
        num_scalar_prefetch=2,
        grid=(nh, s // ATT_TQ),
        in_specs=[
            pl.BlockSpec((ATT_TQ, HEAD_DIM), lambda h, i, cs, jf: (i, h)),
            pl.BlockSpec((s, HEAD_DIM), lambda h, i, cs, jf: (0, nh + h)),
            pl.BlockSpec((s, HEAD_DIM), lambda h, i, cs, jf: (0, 2 * nh + h)),
            pl.BlockSpec((s, LANES), lambda h, i, cs, jf: (0, 0)),
            pl.BlockSpec((ATT_TQ, HEAD_DIM), lambda h, i, cs, jf: (i, 3 * nh + h)),
        ],
        out_specs=pl.BlockSpec((ATT_TQ, HEAD_DIM), lambda h, i, cs, jf: (i, h)),
        scratch_shapes=[
            pltpu.VMEM((nkb, HEAD_DIM, ATT_TK), BF16),
            pltpu.VMEM((2 * HEAD_DIM, ATT_TQ), BF16),
            pltpu.VMEM((2, ATT_GROUPS, ATT_TK, ATT_TG), F32),
            pltpu.VMEM((2, ATT_GROUPS, 1, ATT_TG), F32),
            pltpu.VMEM((HEAD_DIM, ATT_TQ), F32),
            pltpu.VMEM((1, ATT_TQ), F32),
            pltpu.VMEM((1, ATT_TQ), F32),
        ],
    )
    return pl.pallas_call(
        functools.partial(_attn_kernel, nkb=nkb),
        grid_spec=grid_spec,
        out_shape=jax.ShapeDtypeStruct((s, ATT_W), BF16),
        compiler_params=pltpu.CompilerParams(
            dimension_semantics=("arbitrary", "arbitrary"),
            vmem_limit_bytes=_vmem_limit(48 * 1024 * 1024),
        ),
        name="attn",
    )(cstart, jfirst, proj, proj, proj, nr, proj)


def _out_kernel(h_ref, ycs_ref, ya_ref, p_ref, wo_ref, gp_ref, wpg_ref, wpp_ref, o_ref):
    ycat = jnp.concatenate([ycs_ref[...], ya_ref[...]], axis=1)
    y = jnp.dot(ycat, wo_ref[...], preferred_element_type=F32)
    inv = lax.rsqrt(jnp.mean(y * y, axis=-1, keepdims=True) + EPS)
    h1 = h_ref[...] + y * inv * gp_ref[...]
    gate = jax.nn.sigmoid(jnp.dot(h1.astype(BF16), wpg_ref[...], preferred_element_type=F32))
    pp = jnp.dot(p_ref[...].astype(BF16), wpp_ref[...], preferred_element_type=F32)
    o_ref[...] = h1 + gate * pp


def _out(h, ycs, yatt, p, wo, gp, wpg, wpp, layer):
    s = h.shape[0]

    def slab(rows, cols):
        return pl.BlockSpec((None, rows, cols), lambda i: (layer, 0, 0), pipeline_mode=pl.Buffered(1))

    return pl.pallas_call(
        _out_kernel,
        grid=(s // OUT_TM,),
        in_specs=[
            pl.BlockSpec((OUT_TM, D_MODEL), lambda i: (i, 0)),
            pl.BlockSpec((OUT_TM, CONV_W + SGU_W), lambda i: (i, 0)),
            pl.BlockSpec((OUT_TM, ATT_W), lambda i: (i, 0)),
            pl.BlockSpec((None, None, OUT_TM, PLE_DIM), lambda i: (layer, 0, i, 0)),
            slab(D_MODEL, D_MODEL),
            slab(1, D_MODEL),
            slab(D_MODEL, D_MODEL),
            slab(PLE_DIM, D_MODEL),
        ],
        out_specs=pl.BlockSpec((OUT_TM, D_MODEL), lambda i: (i, 0)),
        out_shape=jax.ShapeDtypeStruct((s, D_MODEL), F32),
        compiler_params=pltpu.CompilerParams(
            dimension_semantics=("arbitrary",),
            vmem_limit_bytes=_vmem_limit(52 * 1024 * 1024),
        ),
        name="out",
    )(h, ycs, yatt, p, wo, gp, wpg, wpp)


def kernel(x, p, norm_pre, w_in, b_f, conv_dw, conv_dw_b, conv_ln_g, conv_ln_b, conv_pw, conv_pw_b,
           sgu_ln_g, sgu_ln_b, sgu_w, sgu_b, w_out, norm_post, w_pg, w_pp):
    bsz, seq, _ = x.shape
    depth = w_in.shape[0]
    assert bsz == 1 and seq % PROJ_TM == 0 and seq % ATT_TQ == 0

    n_att = 4 * ATT_W
    colscale = jnp.concatenate(
        [jnp.full((1, ATT_W), HEAD_DIM ** -0.5 * LOG2E, F32), jnp.ones((1, N_MAIN - ATT_W), F32)], axis=1)
    tri = jnp.tril(jnp.ones((FG_T, FG_T), BF16))
    row = lambda v: v.reshape(1, -1)

    w_att = w_in.astype(BF16)
    w_mix = w_att[:, :, n_att + ATT_HEADS:]
    w_f = jnp.pad(w_att[:, :, n_att:n_att + ATT_HEADS], ((0, 0), (0, 0), (0, LANES - ATT_HEADS)))
    w_out_b, w_pg_b, w_pp_b, conv_pw_b16 = (w.astype(BF16) for w in (w_out, w_pg, w_pp, conv_pw))
    norm_post_r = norm_post[:, None, :]

    h = x[0]
    for i in range(depth):
        bf_pad = jnp.pad(b_f[i], (0, LANES - ATT_HEADS)).reshape(1, LANES)
        bsb = jnp.broadcast_to(sgu_b[i][:, :, None], (SGU_HEADS, CHUNK, HEAD_DIM))

        proj, flog = _proj(h, row(norm_pre[i]), w_att, w_mix, w_f, colscale, i)
        nr, cs, jf = _forget(flog, bf_pad, tri, proj)
        cstart = jnp.transpose(cs[:, :CS_ROWS, :ATT_HEADS], (1, 2, 0)).reshape(-1)
        jfirst = jf[:, :ATT_HEADS].T.reshape(-1)
        ycs = _branch(proj, conv_dw[i], row(conv_dw_b[i]), row(conv_ln_g[i]), row(conv_ln_b[i]),
                      conv_pw_b16[i], row(conv_pw_b[i]),
                      row(sgu_ln_g[i]), row(sgu_ln_b[i]), sgu_w[i], bsb)
        yatt = _attn(cstart, jfirst, proj, nr)
        h = _out(h, ycs, yatt, p, w_out_b, norm_post_r, w_pg_b, w_pp_b, i)
    return h[None]
```
